```python
import jax, jax.numpy as jnp
from jax import lax
import numpy as np

D_MODEL = 1024
BATCH = 8
SEQ = 2048
DEPTH = 1
DEC_BATCH = 128
DEC_SEQ = 8
PAST_LEN = 16384
PAGE_SIZE = 128

POOL_WIDTH = D_MODEL
POOL_WINDOWS = (2, 4, 8, 16)
POOL_GROUPS = len(POOL_WINDOWS)
POOL_GROUP_DIM = POOL_WIDTH // POOL_GROUPS
POOL_BUF = max(POOL_WINDOWS) - 1
LRU_WIDTH = D_MODEL
LRU_HEADS = 8
LRU_HEAD_DIM = LRU_WIDTH // LRU_HEADS
CONV_WIDTH = 4
LRU_C = 8.0
N_BRANCH = 2
IN_WIDTH = POOL_WIDTH + 2 * LRU_WIDTH + N_BRANCH * D_MODEL
PEER_HEADS = 8
PEER_NKEYS = 128
PEER_EXPERTS = PEER_NKEYS * PEER_NKEYS
PEER_TOPK = 16
PEER_DKEY = 256
PEER_DHALF = PEER_DKEY // 2
PEER_BLOCK = 256

NORM_EPS = 1e-6
F32 = jnp.float32

kernel_name = "hybrid_pool_rglru_peer_adaln_step"


def rmsnorm(x, g):
    xf = x.astype(F32)
    y = xf * lax.rsqrt(jnp.mean(xf * xf, axis=-1, keepdims=True) + NORM_EPS) * g.astype(F32)
    return y.astype(x.dtype)


def modulate(h, shift, scale):
    return h * (1 + scale[:, None, :]) + shift[:, None, :]


def lru_scan(a, bx, h0):
    def step(h, ab):
        a_t, b_t = ab
        h = a_t * h + b_t
        return h, h
    h_last, hs = lax.scan(step, h0, (jnp.swapaxes(a, 0, 1), jnp.swapaxes(bx, 0, 1)))
    return jnp.swapaxes(hs, 0, 1), h_last


def mixer_block(h, pos, pool_buf, conv_buf, lru_h, w_in, b_gate, pool_mix, pool_scale,
                conv_w, conv_b, w_rg, b_rg, w_ig, b_ig, lru_lambda,
                w_branch_a, w_branch_b, w_out):
    B, T, _ = h.shape
    proj = h @ w_in
    u, xb, yb, gate_logits = jnp.split(
        proj, [POOL_WIDTH, POOL_WIDTH + LRU_WIDTH, POOL_WIDTH + 2 * LRU_WIDTH], axis=-1)

    ext = jnp.concatenate([pool_buf.astype(u.dtype), u], axis=1)
    cs = jnp.cumsum(ext.astype(F32), axis=1)
    cs = jnp.concatenate([jnp.zeros((B, 1, POOL_WIDTH), F32), cs], axis=1)
    end = cs[:, POOL_BUF + 1:]
    means = []
    for g, w in enumerate(POOL_WINDOWS):
        sl = slice(g * POOL_GROUP_DIM, (g + 1) * POOL_GROUP_DIM)
        start = cs[:, POOL_BUF + 1 - w:POOL_BUF + 1 - w + T, sl]
        cnt = jnp.minimum(pos + 1, w).astype(F32)[None, :, None]
        means.append((end[..., sl] - start) / cnt)
    pooled = (jnp.concatenate(means, axis=-1) - u.astype(F32)).astype(h.dtype)
    pooled = pooled.reshape(B, T, POOL_GROUPS, POOL_GROUP_DIM)
    z = jnp.einsum('btgi,gij->btgj', pooled, pool_mix).reshape(B, T, POOL_WIDTH) * pool_scale
    y_a = z @ w_branch_a
    new_pool_buf = ext[:, -POOL_BUF:]

    ext_c = jnp.concatenate([conv_buf.astype(xb.dtype), xb], axis=1)
    xc = conv_b + ext_c[:, 0:T] * conv_w[0]
    for k in range(1, CONV_WIDTH):
        xc = xc + ext_c[:, k:k + T] * conv_w[k]
    new_conv_buf = ext_c[:, -(CONV_WIDTH - 1):]
    xh = xc.reshape(B, T, LRU_HEADS, LRU_HEAD_DIM)
    r = jax.nn.sigmoid((jnp.einsum('bthi,hij->bthj', xh, w_rg).reshape(B, T, LRU_WIDTH) + b_rg).astype(F32))
    i = jax.nn.sigmoid((jnp.einsum('bthi,hij->bthj', xh, w_ig).reshape(B, T, LRU_WIDTH) + b_ig).astype(F32))
    log_a = -LRU_C * r * jax.nn.softplus(-lru_lambda.astype(F32))
    a = jnp.exp(log_a)
    mult = jnp.sqrt(-jnp.expm1(2.0 * log_a))
    mult = jnp.where((pos == 0)[None, :, None], 1.0, mult)
    bx = mult * i * xc.astype(F32)
    hs, h_last = lru_scan(a, bx, lru_h.astype(F32))
    y_b = (hs.astype(h.dtype) * jax.nn.gelu(yb)) @ w_branch_b

    gates = jax.nn.sigmoid((gate_logits + b_gate).astype(F32)).astype(h.dtype)
    g_a, g_b = jnp.split(gates, N_BRANCH, axis=-1)
    out = (g_a * y_a + g_b * y_b) @ w_out
    return out, new_pool_buf, new_conv_buf, h_last


def peer_block(h, w_q, sub_keys, expert_u, expert_v):
    B, T, D = h.shape
    n = B * T
    n_blk = -(-n // PEER_BLOCK)
    flat = jnp.pad(h.reshape(n, D), ((0, n_blk * PEER_BLOCK - n), (0, 0)))
    flat = flat.reshape(n_blk, PEER_BLOCK, D)
    keys = sub_keys.astype(F32)

    def one_block(xb):
        q = (xb @ w_q).reshape(PEER_BLOCK, PEER_HEADS, 2, PEER_DHALF).astype(F32)
        s = jnp.einsum('nhpd,hpkd->nhpk', q, keys)
        top_s, top_i = lax.top_k(s, PEER_TOPK)
        cand = top_s[:, :, 0, :, None] + top_s[:, :, 1, None, :]
        cand_idx = top_i[:, :, 0, :, None] * PEER_NKEYS + top_i[:, :, 1, None, :]
        cand = cand.reshape(PEER_BLOCK, PEER_HEADS, PEER_TOPK * PEER_TOPK)
        cand_idx = cand_idx.reshape(PEER_BLOCK, PEER_HEADS, PEER_TOPK * PEER_TOPK)
        best_s, best_j = lax.top_k(cand, PEER_TOPK)
        idx = jnp.take_along_axis(cand_idx, best_j, axis=-1)
        gsm = jax.nn.softmax(best_s, axis=-1)
        act = jax.nn.gelu(jnp.einsum('nd,nhkd->nhk', xb, expert_u[idx]).astype(F32))
        wts = (gsm * act).astype(xb.dtype)
        return jnp.einsum('nhk,nhkd->nd', wts, expert_v[idx])

    out = lax.map(one_block, flat)
    return out.reshape(n_blk * PEER_BLOCK, D)[:n].reshape(B, T, D)


def run_trunk(x, c, pos, pool_bufs, conv_bufs, lru_hs, norm1, norm2, w_ada, b_ada, w_in, b_gate,
              pool_mix, pool_scale, conv_w, conv_b, w_rg, b_rg, w_ig, b_ig, lru_lambda,
              w_branch_a, w_branch_b, w_out, w_q, sub_keys, expert_u, expert_v, final_norm):
    new_pool, new_conv, new_lru = [], [], []
    for l in range(DEPTH):
        mod = jax.nn.silu(c) @ w_ada[l] + b_ada[l]
        sh1, sc1, g1, sh2, sc2, g2 = jnp.split(mod, 6, axis=-1)
        h = modulate(rmsnorm(x, norm1[l]), sh1, sc1)
        mix, pb, cb, lh = mixer_block(
            h, pos, pool_bufs[l], conv_bufs[l], lru_hs[l], w_in[l], b_gate[l], pool_mix[l],
            pool_scale[l], conv_w[l], conv_b[l], w_rg[l], b_rg[l], w_ig[l], b_ig[l],
            lru_lambda[l], w_branch_a[l], w_branch_b[l], w_out[l])
        x = x + g1[:, None, :] * mix
        h = modulate(rmsnorm(x, norm2[l]), sh2, sc2)
        x = x + g2[:, None, :] * peer_block(h, w_q[l], sub_keys[l], expert_u[l], expert_v[l])
        new_pool.append(pb)
        new_conv.append(cb)
        new_lru.append(lh)
    y = rmsnorm(x, final_norm)
    return y, jnp.stack(new_pool), jnp.stack(new_conv), jnp.stack(new_lru)


def setup_inputs(seed: int = 0) -> dict:
    key = jax.random.key(seed)
    ks = jax.random.split(key, 32)
    nrm = lambda k, shape, s: jax.random.normal(k, shape, F32) * s
    L, D = DEPTH, D_MODEL
    target = jax.random.uniform(ks[20], (L, LRU_WIDTH), F32, 0.9, 0.999)
    a_base = target ** (1.0 / LRU_C)
    lru_lambda = jnp.log(a_base) - jnp.log1p(-a_base)
    return {
        "x_prompt": nrm(ks[0], (BATCH, SEQ, D), 1.0),
        "x_sample": nrm(ks[1], (DEC_BATCH, DEC_SEQ, D), 1.0),
        "c_prompt": nrm(ks[2], (BATCH, D), 1.0),
        "c_sample": nrm(ks[3], (DEC_BATCH, D), 1.0),
        "state_pool": nrm(ks[4], (L, DEC_BATCH, POOL_BUF, POOL_WIDTH), 1.0),
        "state_conv": nrm(ks[5], (L, DEC_BATCH, CONV_WIDTH - 1, LRU_WIDTH), 1.0),
        "state_lru": nrm(ks[6], (L, DEC_BATCH, LRU_WIDTH), 0.5),
        "norm1": 1.0 + nrm(ks[7], (L, D), 0.02),
        "norm2": 1.0 + nrm(ks[8], (L, D), 0.02),
        "w_ada": nrm(ks[9], (L, D, 6 * D), 0.5 * D ** -0.5),
        "b_ada": nrm(ks[10], (L, 6 * D), 0.02),
        "w_in": nrm(ks[11], (L, D, IN_WIDTH), D ** -0.5),
        "b_gate": nrm(ks[12], (L, N_BRANCH * D), 0.02),
        "pool_mix": nrm(ks[13], (L, POOL_GROUPS, POOL_GROUP_DIM, POOL_GROUP_DIM), POOL_GROUP_DIM ** -0.5),
        "pool_scale": 0.5 + nrm(ks[14], (L, POOL_WIDTH), 0.05),
        "conv_w": nrm(ks[15], (L, CONV_WIDTH, LRU_WIDTH), CONV_WIDTH ** -0.5),
        "conv_b": nrm(ks[16], (L, LRU_WIDTH), 0.02),
        "w_rg": nrm(ks[17], (L, LRU_HEADS, LRU_HEAD_DIM, LRU_HEAD_DIM), LRU_HEAD_DIM ** -0.5),
        "b_rg": nrm(ks[18], (L, LRU_WIDTH), 0.02),
        "w_ig": nrm(ks[19], (L, LRU_HEADS, LRU_HEAD_DIM, LRU_HEAD_DIM), LRU_HEAD_DIM ** -0.5),
        "b_ig": nrm(ks[21], (L, LRU_WIDTH), 0.02),
        "lru_lambda": lru_lambda,
        "w_branch_a": nrm(ks[22], (L, POOL_WIDTH, D), POOL_WIDTH ** -0.5),
        "w_branch_b": nrm(ks[23], (L, LRU_WIDTH, D), LRU_WIDTH ** -0.5),
        "w_out": nrm(ks[24], (L, D, D), D ** -0.5),
        "w_q": nrm(ks[25], (L, D, PEER_HEADS * PEER_DKEY), D ** -0.5),
        "sub_keys": nrm(ks[26], (L, PEER_HEADS, 2, PEER_NKEYS, PEER_DHALF), PEER_DHALF ** -0.5),
        "expert_u": nrm(ks[27], (L, PEER_EXPERTS, D), D ** -0.5),
        "expert_v": nrm(ks[28], (L, PEER_EXPERTS, D), 0.3),
        "final_norm": 1.0 + nrm(ks[29], (D,), 0.02),
    }


def reference(x_prompt, x_sample, c_prompt, c_sample, state_pool, state_conv, state_lru,
              norm1, norm2, w_ada, b_ada, w_in, b_gate, pool_mix, pool_scale, conv_w, conv_b,
              w_rg, b_rg, w_ig, b_ig, lru_lambda, w_branch_a, w_branch_b, w_out,
              w_q, sub_keys, expert_u, expert_v, final_norm):
    weights = (norm1, norm2, w_ada, b_ada, w_in, b_gate, pool_mix, pool_scale, conv_w, conv_b,
               w_rg, b_rg, w_ig, b_ig, lru_lambda, w_branch_a, w_branch_b, w_out,
               w_q, sub_keys, expert_u, expert_v, final_norm)
    pos_p = jnp.arange(SEQ, dtype=jnp.int32)
    zp = jnp.zeros((DEPTH, BATCH, POOL_BUF, POOL_WIDTH), x_prompt.dtype)
    zc = jnp.zeros((DEPTH, BATCH, CONV_WIDTH - 1, LRU_WIDTH), x_prompt.dtype)
    zh = jnp.zeros((DEPTH, BATCH, LRU_WIDTH), F32)
    y_prompt, pool_p, conv_p, lru_p = run_trunk(x_prompt, c_prompt, pos_p, zp, zc, zh, *weights)
    pos_s = PAST_LEN + jnp.arange(DEC_SEQ, dtype=jnp.int32)
    y_sample, pool_s, conv_s, lru_s = run_trunk(x_sample, c_sample, pos_s, state_pool, state_conv,
                                                state_lru, *weights)
    lru_p = lru_p.astype(state_lru.dtype)
    lru_s = lru_s.astype(state_lru.dtype)
    return (y_prompt, y_sample, pool_p, conv_p, lru_p, pool_s, conv_s, lru_s)
```

```python
import functools
import math

import jax
import jax.numpy as jnp
from jax import lax
from jax.experimental import pallas as pl
from jax.experimental.pallas import tpu as pltpu

F32 = jnp.float32
BF16 = jnp.bfloat16
I32 = jnp.int32

NORM_EPS = 1e-6
PAST_LEN = 16384
LRU_C = 8.0
POOL_WINDOWS = (2, 4, 8, 16)
POOL_HIST = 16
CONV_WIDTH = 4
CONV_HIST = 8
LRU_HEADS = 8
PEER_HEADS = 8
PEER_NKEYS = 128
PEER_TOPK = 16
SUBLANES = 8
LANES = 128
VMEM_LIMIT = 56 * 1024 * 1024

EXP_TOK = 256
EXP_CHUNK = 2048
EXP_SUB = 256
S_PITCH = 136
ROUTE_TOK = 512


def _dot(a, b):
    return jnp.dot(a, b, preferred_element_type=F32)


def _dot_nt(a, b):
    return lax.dot_general(a, b, (((1,), (1,)), ((), ())), preferred_element_type=F32)


def _rmsnorm(x, g):
    return x * lax.rsqrt(jnp.mean(x * x, axis=-1, keepdims=True) + NORM_EPS) * g


def _gelu(x):
    return jax.nn.gelu(x, approximate=True)


def _ada_kernel(c_ref, w_ref, b_ref, o_ref):
    c = c_ref[...]
    s = (c * jax.nn.sigmoid(c)).astype(BF16)
    o_ref[...] = _dot(s, w_ref[...].astype(BF16)) + b_ref[...]


def _ada(c, w_ada, b_ada):
    n, d = c.shape
    dout = w_ada.shape[1]
    tn = 1024
    return pl.pallas_call(
        _ada_kernel,
        grid=(dout // tn,),
        in_specs=[
            pl.BlockSpec((n, d), lambda j: (0, 0)),
            pl.BlockSpec((d, tn), lambda j: (0, j)),
            pl.BlockSpec((1, tn), lambda j: (0, j)),
        ],
        out_specs=pl.BlockSpec((n, tn), lambda j: (0, j)),
        out_shape=jax.ShapeDtypeStruct((n, dout), F32),
        compiler_params=pltpu.CompilerParams(
            dimension_semantics=("arbitrary",), vmem_limit_bytes=VMEM_LIMIT),
        name="ada",
    )(c, w_ada, b_ada.reshape(1, dout))


def _mixer_kernel(x_ref, sh_ref, sc_ref, g1_ref, pool0_ref, conv0_ref, h0_ref,
                  norm1_ref, w_in_ref, b_gate_ref, pool_mix_ref, pool_scale_ref,
                  conv_w_ref, conv_b_ref, w_gi_ref, b_rg_ref, b_ig_ref, lam_ref,
                  w_a_ref, w_b_ref, w_out_ref,
                  x1_ref, pool_out_ref, conv_out_ref, h_out_ref,
                  ext_u, ext_c, gate_s, a_s, b_s, hs_s, hcar,
                  *, bb, seg, pos0, width):
    rows = bb * seg
    tt = pl.program_id(1)
    w = width
    gdim = w // len(POOL_WINDOWS)
    hdim = w // LRU_HEADS

    @pl.when(tt == 0)
    def _():
        ext_u[:, 0:POOL_HIST, :] = pool0_ref[...]
        ext_c[:, 0:CONV_HIST, :] = conv0_ref[...]
        hcar[...] = h0_ref[...].reshape(bb * SUBLANES, w)

    x = x_ref[...].reshape(rows, w)
    h = _rmsnorm(x, norm1_ref[...]) * (1.0 + sc_ref[0]) + sh_ref[0]
    hb = h.astype(BF16)

    row = lax.broadcasted_iota(I32, (rows, 1), 0)
    pos = pos0 + tt * seg + (row & (seg - 1))

    u = _dot(hb, w_in_ref[:, 0:w])
    ext_u[:, POOL_HIST:POOL_HIST + seg, :] = u.reshape(bb, seg, w)
    for g, win in enumerate(POOL_WINDOWS):
        lo = g * gdim
        acc = u[:, lo:lo + gdim]
        for j in range(1, win):
            acc = acc + ext_u[:, POOL_HIST - j:POOL_HIST - j + seg, lo:lo + gdim].reshape(rows, gdim)
        cnt = jnp.minimum(pos + 1, win).astype(F32)
        pooled = acc / cnt - u[:, lo:lo + gdim]
        zg = _dot(pooled.astype(BF16), pool_mix_ref[g])
        gate_s[:, lo:lo + gdim] = zg
    z = gate_s[...] * pool_scale_ref[...]
    y_a = _dot(z.astype(BF16), w_a_ref[...])
    pool_out_ref[...] = ext_u[:, seg:seg + POOL_HIST, :]
    ext_u[:, 0:POOL_HIST, :] = ext_u[:, seg:seg + POOL_HIST, :]

    gl = _dot(hb, w_in_ref[:, 3 * w:5 * w]) + b_gate_ref[...]
    gates = jax.nn.sigmoid(gl)
    m = gates[:, 0:w] * y_a

    xb = _dot(hb, w_in_ref[:, w:2 * w])
    ext_c[:, CONV_HIST:CONV_HIST + seg, :] = xb.reshape(bb, seg, w)
    xc = conv_b_ref[...] + xb * conv_w_ref[CONV_WIDTH - 1:CONV_WIDTH, :]
    for k in range(CONV_WIDTH - 1):
        sh = CONV_WIDTH - 1 - k
        xc = xc + ext_c[:, CONV_HIST - sh:CONV_HIST - sh + seg, :].reshape(rows, w) * conv_w_ref[k:k + 1, :]
    conv_out_ref[...] = ext_c[:, seg:seg + CONV_HIST, :]
    ext_c[:, 0:CONV_HIST, :] = ext_c[:, seg:seg + CONV_HIST, :]

    xcb = xc.astype(BF16)
    lam = -lam_ref[...]
    softplus = jnp.maximum(lam, 0.0) + jnp.log1p(jnp.exp(-jnp.abs(lam)))
    for hh in range(LRU_HEADS):
        lo = hh * hdim
        ri = _dot(xcb[:, lo:lo + hdim], w_gi_ref[hh])
        r = jax.nn.sigmoid(ri[:, 0:hdim] + b_rg_ref[:, lo:lo + hdim])
        i = jax.nn.sigmoid(ri[:, hdim:2 * hdim] + b_ig_ref[:, lo:lo + hdim])
        log_a = (-LRU_C) * r * softplus[:, lo:lo + hdim]
        a = jnp.exp(log_a)
        mult = jnp.sqrt(-jnp.tanh(log_a) * (a * a + 1.0))
        mult = jnp.where(pos == 0, 1.0, mult)
        a_s[:, lo:lo + hdim] = a
        b_s[:, lo:lo + hdim] = mult * i * xc[:, lo:lo + hdim]

    a = a_s[...]
    b = b_s[...]
    rin = row & (SUBLANES - 1)
    for d in (1, 2, 4):
        a_sh = jnp.where(rin >= d, pltpu.roll(a, d, 0), 1.0)
        b_sh = jnp.where(rin >= d, pltpu.roll(b, d, 0), 0.0)
        b = a * b_sh + b
        a = a * a_sh
    if seg == SUBLANES:
        hs = a * hcar[...] + b
        h_out_ref[...] = hs.reshape(bb, SUBLANES, w)
    else:
        a_s[...] = a
        b_s[...] = b

        def tile_step(k, hp):
            r0 = pl.multiple_of(k * SUBLANES, SUBLANES)
            ht = a_s[pl.ds(r0, SUBLANES), :] * hp + b_s[pl.ds(r0, SUBLANES), :]
            hs_s[pl.ds(r0, SUBLANES), :] = ht
            return jnp.broadcast_to(ht[SUBLANES - 1:SUBLANES, :], (SUBLANES, w))

        hp = lax.fori_loop(0, rows // SUBLANES, tile_step, hcar[...])
        hcar[...] = hp
        h_out_ref[...] = hp.reshape(bb, SUBLANES, w)
        hs = hs_s[...]

    yb = _dot(hb, w_in_ref[:, 2 * w:3 * w])
    y_b = _dot((hs * _gelu(yb)).astype(BF16), w_b_ref[...])
    m = m + gates[:, w:2 * w] * y_b
    out = _dot(m.astype(BF16), w_out_ref[...])
    x1_ref[...] = (x + g1_ref[0] * out).reshape(bb, seg, w)


def _const_spec(shape):
    nd = len(shape)
    return pl.BlockSpec(shape, lambda *_: (0,) * nd, pipeline_mode=pl.Buffered(1))


def _mixer(x, sh, sc, g1, pool0, conv0, h0e, weights, *, bb, seg, pos0, per_token_mod):
    bsz, t, d = x.shape
    w = d
    rows = bb * seg
    nb, nt = bsz // bb, t // seg
    rm = sh.shape[1]

    def mod_map(b, tt):
        return ((b * nt + tt) if per_token_mod else b, 0, 0)

    mod_spec = pl.BlockSpec((1, rm, d), mod_map)
    wnames = ("norm1", "w_in", "b_gate", "pool_mix", "pool_scale", "conv_w", "conv_b",
              "w_gi", "b_rg", "b_ig", "lam", "w_a", "w_b", "w_out")
    wvals = [weights[k] for k in wnames]
    kern = functools.partial(_mixer_kernel, bb=bb, seg=seg, pos0=pos0, width=w)
    return pl.pallas_call(
        kern,
        grid=(nb, nt),
        in_specs=[
            pl.BlockSpec((bb, seg, d), lambda b, tt: (b, tt, 0)),
            mod_spec, mod_spec, mod_spec,
            pl.BlockSpec((bb, POOL_HIST, w), lambda b, tt: (b, 0, 0)),
            pl.BlockSpec((bb, CONV_HIST, w), lambda b, tt: (b, 0, 0)),
            pl.BlockSpec((bb, SUBLANES, w), lambda b, tt: (b, 0, 0)),
        ] + [_const_spec(v.shape) for v in wvals],
        out_specs=[
            pl.BlockSpec((bb, seg, d), lambda b, tt: (b, tt, 0)),
            pl.BlockSpec((bb, POOL_HIST, w), lambda b, tt: (b, 0, 0)),
            pl.BlockSpec((bb, CONV_HIST, w), lambda b, tt: (b, 0, 0)),
            pl.BlockSpec((bb, SUBLANES, w), lambda b, tt: (b, 0, 0)),
        ],
        out_shape=[
            jax.ShapeDtypeStruct((bsz, t, d), F32),
            jax.ShapeDtypeStruct((bsz, POOL_HIST, w), F32),
            jax.ShapeDtypeStruct((bsz, CONV_HIST, w), F32),
            jax.ShapeDtypeStruct((bsz, SUBLANES, w), F32),
        ],
        scratch_shapes=[
            pltpu.VMEM((bb, POOL_HIST + seg, w), F32),
            pltpu.VMEM((bb, CONV_HIST + seg, w), F32),
            pltpu.VMEM((rows, w), F32),
            pltpu.VMEM((rows, w), F32),
            pltpu.VMEM((rows, w), F32),
            pltpu.VMEM((rows, w), F32),
            pltpu.VMEM((bb * SUBLANES, w), F32),
        ],
        compiler_params=pltpu.CompilerParams(
            dimension_semantics=("arbitrary", "arbitrary"), vmem_limit_bytes=VMEM_LIMIT),
        name="mixer",
    )(x, sh, sc, g1, pool0, conv0, h0e, *wvals)


def _top16_rows(s, val_ref, idx_ref):
    nk = s.shape[0]
    iota = lax.broadcasted_iota(I32, s.shape, 0).astype(F32)
    for it in range(PEER_TOPK):
        mx = jnp.max(s, axis=0, keepdims=True)
        ix = jnp.min(jnp.where(s == mx, iota, float(nk)), axis=0, keepdims=True)
        val_ref[it:it + 1, :] = mx
        idx_ref[it:it + 1, :] = ix
        s = jnp.where(iota == ix, -jnp.inf, s)


def _route_kernel(x_ref, sh_ref, sc_ref, norm2_ref, wq_ref, keys_ref,
                  h2_ref, e_ref, g_ref, h2_s, q_s, v1_s, i1_s, v2_s, i2_s, bs_s, be_s, *, ntok):
    hd = pl.program_id(1)

    @pl.when(hd == 0)
    def _():
        x = x_ref[...]
        h2 = _rmsnorm(x, norm2_ref[...]) * (1.0 + sc_ref[0]) + sh_ref[0]
        h2b = h2.astype(BF16)
        h2_s[...] = h2b
        h2_ref[...] = h2b

    q_s[...] = _dot(h2_s[...], wq_ref[...])
    k1 = keys_ref[0, 0]
    k2 = keys_ref[0, 1]
    half = PEER_NKEYS

    riota = lax.broadcasted_iota(I32, (SUBLANES, LANES), 0).astype(F32)
    nkf = float(PEER_NKEYS)
    topf = float(PEER_TOPK)

    def sub_block(j, carry):
        r0 = pl.multiple_of(j * LANES, LANES)
        qj = q_s[pl.ds(r0, LANES), :].astype(BF16)
        _top16_rows(_dot_nt(k1, qj[:, 0:half]), v1_s, i1_s)
        _top16_rows(_dot_nt(k2, qj[:, half:2 * half]), v2_s, i2_s)
        s2_lo = v2_s[0:SUBLANES, :]
        s2_hi = v2_s[SUBLANES:2 * SUBLANES, :]
        i2_lo = i2_s[0:SUBLANES, :]
        i2_hi = i2_s[SUBLANES:2 * SUBLANES, :]
        cands, eids, flats = [], [], []
        v1a = v1_s[0:1, :]
        e1a = i1_s[0:1, :] * nkf
        cands += [v1a + s2_lo, v1a + s2_hi]
        eids += [e1a + i2_lo, e1a + i2_hi]
        flats += [riota, riota + float(SUBLANES)]
        for a in range(1, SUBLANES):
            nb = PEER_TOPK // (a + 1)
            c = v1_s[a:a + 1, :] + s2_lo
            if nb < SUBLANES:
                c = jnp.where(riota < float(nb), c, -jnp.inf)
            cands.append(c)
            eids.append(i1_s[a:a + 1, :] * nkf + i2_lo)
            flats.append(riota + a * topf)
        cands.append(v1_s[SUBLANES:2 * SUBLANES, :] + v2_s[0:1, :])
        eids.append(i1_s[SUBLANES:2 * SUBLANES, :] * nkf + i2_s[0:1, :])
        flats.append((riota + float(SUBLANES)) * topf)
        cand = jnp.concatenate(cands, axis=0)
        eid = jnp.concatenate(eids, axis=0)
        flat = jnp.concatenate(flats, axis=0)
        for it in range(PEER_TOPK):
            mx = jnp.max(cand, axis=0, keepdims=True)
            fx = jnp.min(jnp.where(cand == mx, flat, topf * topf), axis=0, keepdims=True)
            hit = flat == fx
            be_s[it:it + 1, :] = jnp.max(jnp.where(hit, eid, -1.0), axis=0, keepdims=True)
            bs_s[it:it + 1, :] = mx
            cand = jnp.where(hit, -jnp.inf, cand)
        best = bs_s[...]
        ex = jnp.exp(best - best[0:1, :])
        gsm = ex / jnp.sum(ex, axis=0, keepdims=True)
        e_ref[j] = be_s[...]
        g_ref[j] = gsm
        return carry

    lax.fori_loop(0, ntok // LANES, sub_block, 0)


def _route(x1, sh2, sc2, norm2, w_q, keys, *, tok_per_mod):
    n, d = x1.shape
    nt = ROUTE_TOK
    tiles = n // nt
    rm = sh2.shape[1]
    tiles_per_mod = tok_per_mod // nt
    mod_spec = pl.BlockSpec((1, rm, d), lambda i, hd: (i // tiles_per_mod, 0, 0))
    dk2 = w_q.shape[1] // PEER_HEADS
    kern = functools.partial(_route_kernel, ntok=nt)
    return pl.pallas_call(
        kern,
        grid=(tiles, PEER_HEADS),
        in_specs=[
            pl.BlockSpec((nt, d), lambda i, hd: (i, 0)),
            mod_spec, mod_spec,
            pl.BlockSpec((1, d), lambda i, hd: (0, 0)),
            pl.BlockSpec((d, dk2), lambda i, hd: (0, hd)),
            pl.BlockSpec((1, 2, PEER_NKEYS, dk2 // 2), lambda i, hd: (hd, 0, 0, 0)),
        ],
        out_specs=[
            pl.BlockSpec((nt, d), lambda i, hd: (i, 0)),
            pl.BlockSpec((nt // LANES, PEER_TOPK, LANES), lambda i, hd: (i, hd, 0)),
            pl.BlockSpec((nt // LANES, PEER_TOPK, LANES), lambda i, hd: (i, hd, 0)),
        ],
        out_shape=[
            jax.ShapeDtypeStruct((n, d), BF16),
            jax.ShapeDtypeStruct((n // LANES, PEER_HEADS * PEER_TOPK, LANES), F32),
            jax.ShapeDtypeStruct((n // LANES, PEER_HEADS * PEER_TOPK, LANES), F32),
        ],
        scratch_shapes=[
            pltpu.VMEM((nt, d), BF16),
            pltpu.VMEM((nt, dk2), F32),
        ] + [pltpu.VMEM((PEER_TOPK, LANES), F32)] * 6,
        compiler_params=pltpu.CompilerParams(
            dimension_semantics=("arbitrary", "arbitrary"), vmem_limit_bytes=VMEM_LIMIT),
        name="route",
    )(x1, sh2, sc2, norm2, w_q, keys)


def _expert_kernel(h2_ref, e_ref, g_ref, u_ref, v_ref, x1_ref, g2_ref, fnorm_ref,
                   y_ref, s_s, erow_s, grow_s, w2_s, acc_s, *, ntok):
    c = pl.program_id(1)
    nchunks = pl.num_programs(1)
    nk = PEER_NKEYS

    @pl.when(c == 0)
    def _():
        acc_s[...] = jnp.zeros_like(acc_s)
        for sb in range(ntok // LANES):
            erow_s[sb * LANES:(sb + 1) * LANES, :] = e_ref[sb].T.astype(I32)
            grow_s[sb * LANES:(sb + 1) * LANES, :] = g_ref[sb].T
        riota = lax.broadcasted_iota(I32, (nk, nk), 0)
        zeros = jnp.zeros((nk, nk), BF16)

        def pair(p, carry):
            n0 = pl.multiple_of(2 * p, 2)
            lts, rts = [], []
            for t in range(2):
                e = erow_s[pl.ds(n0 + t, 1), :]
                gv = grow_s[pl.ds(n0 + t, 1), :]
                i1 = e >> 7
                i2 = e & (nk - 1)
                lts.append(jnp.where(riota == i1, gv, 0.0).astype(BF16))
                rts.append(jnp.where(riota == i2, 1.0, 0.0).astype(BF16))
            lhs = jnp.concatenate(lts, axis=1)
            rhs_t = jnp.concatenate([jnp.concatenate([rts[0], zeros], axis=1),
                                     jnp.concatenate([zeros, rts[1]], axis=1)], axis=0)
            gp = _dot_nt(lhs, rhs_t)
            for t in range(2):
                base = pl.multiple_of((n0 + t) * S_PITCH, SUBLANES)
                s_s[pl.ds(base, nk), :] = gp[:, t * nk:(t + 1) * nk]
            return carry

        lax.fori_loop(0, ntok // 2, pair, 0)

    h2 = h2_ref[...]
    i1_per_sub = EXP_SUB // nk
    for j in range(EXP_CHUNK // EXP_SUB):
        a = _dot_nt(h2, u_ref[j * EXP_SUB:(j + 1) * EXP_SUB, :])
        i1_0 = c * (EXP_CHUNK // nk) + j * i1_per_sub
        gs = [s_s[pl.ds(i1_0 + t, ntok, stride=S_PITCH), :] for t in range(i1_per_sub)]
        g = jnp.concatenate(gs, axis=1)
        w2_s[:, j * EXP_SUB:(j + 1) * EXP_SUB] = (_gelu(a) * g).astype(BF16)
    acc_s[...] += _dot(w2_s[...], v_ref[...])

    @pl.when(c == nchunks - 1)
    def _():
        x2 = x1_ref[...] + g2_ref[0] * acc_s[...]
        y_ref[...] = _rmsnorm(x2, fnorm_ref[...])


def _expert(h2, eT, gT, u_bf, v_bf, x1, g2, fnorm, *, tok_per_mod):
    n, d = x1.shape
    nt = EXP_TOK
    ne = u_bf.shape[0]
    rm = g2.shape[1]
    blocks_per_mod = tok_per_mod // nt
    kern = functools.partial(_expert_kernel, ntok=nt)
    return pl.pallas_call(
        kern,
        grid=(n // nt, ne // EXP_CHUNK),
        in_specs=[
            pl.BlockSpec((nt, d), lambda i, c: (i, 0)),
            pl.BlockSpec((nt // LANES, PEER_HEADS * PEER_TOPK, LANES), lambda i, c: (i, 0, 0)),
            pl.BlockSpec((nt // LANES, PEER_HEADS * PEER_TOPK, LANES), lambda i, c: (i, 0, 0)),
            pl.BlockSpec((EXP_CHUNK, d), lambda i, c: (c, 0)),
            pl.BlockSpec((EXP_CHUNK, d), lambda i, c: (c, 0)),
            pl.BlockSpec((nt, d), lambda i, c: (i, 0)),
            pl.BlockSpec((1, rm, d), lambda i, c: (i // blocks_per_mod, 0, 0)),
            pl.BlockSpec((1, d), lambda i, c: (0, 0)),
        ],
        out_specs=pl.BlockSpec((nt, d), lambda i, c: (i, 0)),
        out_shape=jax.ShapeDtypeStruct((n, d), F32),
        scratch_shapes=[
            pltpu.VMEM((nt * S_PITCH, LANES), F32),
            pltpu.VMEM((nt, PEER_HEADS * PEER_TOPK), I32),
            pltpu.VMEM((nt, PEER_HEADS * PEER_TOPK), F32),
            pltpu.VMEM((nt, EXP_CHUNK), BF16),
            pltpu.VMEM((nt, d), F32),
        ],
        compiler_params=pltpu.CompilerParams(
            dimension_semantics=("arbitrary", "arbitrary"), vmem_limit_bytes=VMEM_LIMIT),
        name="expert",
    )(h2, eT, gT, u_bf, v_bf, x1, g2, fnorm)


def _run_group(x, mod, pool0, conv0, h0, wts, *, bb, seg, pos0, per_token_mod):
    bsz, t, d = x.shape
    sh1, sc1, g1, sh2, sc2, g2 = jnp.split(mod, 6, axis=-1)
    n = bsz * t

    def expand(m, tile_rows):
        if per_token_mod:
            return jnp.broadcast_to(m[:, None, :], (bsz, t, d)).reshape(n // tile_rows, tile_rows, d)
        return m[:, None, :]

    pool0p = jnp.pad(pool0, ((0, 0), (POOL_HIST - pool0.shape[1], 0), (0, 0)))
    conv0p = jnp.pad(conv0, ((0, 0), (CONV_HIST - conv0.shape[1], 0), (0, 0)))
    h0e = jnp.broadcast_to(h0[:, None, :], (bsz, SUBLANES, h0.shape[-1]))
    rows = bb * seg
    x1, pool_o, conv_o, h_o = _mixer(
        x, expand(sh1, rows), expand(sc1, rows), expand(g1, rows), pool0p, conv0p, h0e, wts,
        bb=bb, seg=seg, pos0=pos0, per_token_mod=per_token_mod)
    x1f = x1.reshape(n, d)
    tok_per_mod_r = ROUTE_TOK if per_token_mod else t
    h2, eT, gT = _route(x1f, expand(sh2, ROUTE_TOK), expand(sc2, ROUTE_TOK), wts["norm2"],
                        wts["w_q"], wts["keys"], tok_per_mod=tok_per_mod_r)
    tok_per_mod_e = EXP_TOK if per_token_mod else t
    y = _expert(h2, eT, gT, wts["u"], wts["v"], x1f, expand(g2, EXP_TOK), wts["fnorm"],
                tok_per_mod=tok_per_mod_e)
    return (y.reshape(bsz, t, d), pool_o[:, 1:], conv_o[:, CONV_HIST - (CONV_WIDTH - 1):],
            h_o[:, SUBLANES - 1])


def kernel(x_prompt, x_sample, c_prompt, c_sample, state_pool, state_conv, state_lru, norm1, norm2, w_ada, b_ada, w_in, b_gate, pool_mix, pool_scale, conv_w, conv_b, w_rg, b_rg, w_ig, b_ig, lru_lambda, w_branch_a, w_branch_b, w_out, w_q, sub_keys, expert_u, expert_v, final_norm):
    depth = w_in.shape[0]
    assert depth == 1, "single-layer trunk"
    bp, tp, d = x_prompt.shape
    bs, ts, _ = x_sample.shape
    l = 0
    mod = _ada(jnp.concatenate([c_prompt, c_sample], axis=0), w_ada[l], b_ada[l])
    row = lambda v: v.reshape(1, -1)
    wts = {
        "norm1": row(norm1[l]), "norm2": row(norm2[l]), "fnorm": row(final_norm),
        "w_in": w_in[l].astype(BF16), "b_gate": row(b_gate[l]),
        "pool_mix": pool_mix[l].astype(BF16), "pool_scale": row(pool_scale[l]),
        "conv_w": conv_w[l], "conv_b": row(conv_b[l]),
        "w_gi": jnp.concatenate([w_rg[l], w_ig[l]], axis=-1).astype(BF16),
        "b_rg": row(b_rg[l]), "b_ig": row(b_ig[l]), "lam": row(lru_lambda[l]),
        "w_a": w_branch_a[l].astype(BF16), "w_b": w_branch_b[l].astype(BF16),
        "w_out": w_out[l].astype(BF16), "w_q": w_q[l].astype(BF16),
        "keys": sub_keys[l].astype(BF16),
        "u": expert_u[l].astype(BF16), "v": expert_v[l].astype(BF16),
    }
    zp = jnp.zeros((bp, state_pool.shape[2], state_pool.shape[3]), F32)
    zc = jnp.zeros((bp, state_conv.shape[2], state_conv.shape[3]), F32)
    zh = jnp.zeros((bp, state_lru.shape[2]), F32)
    y_p, pool_p, conv_p, lru_p = _run_group(
        x_prompt, mod[:bp], zp, zc, zh, wts, bb=1, seg=256, pos0=0, per_token_mod=False)
    y_s, pool_s, conv_s, lru_s = _run_group(
        x_sample, mod[bp:], state_pool[l], state_conv[l], state_lru[l], wts,
        bb=32, seg=ts, pos0=PAST_LEN, per_token_mod=True)
    st = lambda v: v[None]
    return (y_p, y_s, st(pool_p), st(conv_p), st(lru_p).astype(state_lru.dtype),
            st(pool_s), st(conv_s), st(lru_s).astype(state_lru.dtype))
```

```python
import functools
import math

import jax
import jax.numpy as jnp
from jax import lax
from jax.experimental import pallas as pl
from jax.experimental.pallas import tpu as pltpu

F32 = jnp.float32
BF16 = jnp.bfloat16
I32 = jnp.int32

NORM_EPS = 1e-6
PAST_LEN = 16384
LRU_C = 8.0
POOL_WINDOWS = (2, 4, 8, 16)
POOL_HIST = 16
CONV_WIDTH = 4
CONV_HIST = 8
LRU_HEADS = 8
PEER_HEADS = 8
PEER_NKEYS = 128
PEER_TOPK = 16
SUBLANES = 8
LANES = 128
VMEM_LIMIT = 56 * 1024 * 1024

EXP_TOK = 256
EXP_CHUNK = 2048
EXP_SUB = 256
PAIR_UNROLL = 8
S_PITCH = 136
ROUTE_TOK = 512


def _dot(a, b):
    return jnp.dot(a, b, preferred_element_type=F32)


def _dot_nt(a, b):
    return lax.dot_general(a, b, (((1,), (1,)), ((), ())), preferred_element_type=F32)


def _rmsnorm(x, g):
    return x * lax.rsqrt(jnp.mean(x * x, axis=-1, keepdims=True) + NORM_EPS) * g


def _gelu(x):
    return jax.nn.gelu(x, approximate=True)


def _ada_kernel(c_ref, w_ref, b_ref, o_ref):
    c = c_ref[...]
    s = (c * jax.nn.sigmoid(c)).astype(BF16)
    o_ref[...] = _dot(s, w_ref[...].astype(BF16)) + b_ref[...]


def _ada(c, w_ada, b_ada):
    n, d = c.shape
    dout = w_ada.shape[1]
    tn = 1024
    return pl.pallas_call(
        _ada_kernel,
        grid=(dout // tn,),
        in_specs=[
            pl.BlockSpec((n, d), lambda j: (0, 0)),
            pl.BlockSpec((d, tn), lambda j: (0, j)),
            pl.BlockSpec((1, tn), lambda j: (0, j)),
        ],
        out_specs=pl.BlockSpec((n, tn), lambda j: (0, j)),
        out_shape=jax.ShapeDtypeStruct((n, dout), F32),
        compiler_params=pltpu.CompilerParams(
            dimension_semantics=("arbitrary",), vmem_limit_bytes=VMEM_LIMIT),
        name="ada",
    )(c, w_ada, b_ada.reshape(1, dout))


def _mixer_kernel(x_ref, sh_ref, sc_ref, g1_ref, pool0_ref, conv0_ref, h0_ref,
                  norm1_ref, w_in_ref, b_gate_ref, pool_mix_ref, pool_scale_ref,
                  conv_w_ref, conv_b_ref, w_gi_ref, b_rg_ref, b_ig_ref, lam_ref,
                  w_a_ref, w_b_ref, w_out_ref,
                  x1_ref, pool_out_ref, conv_out_ref, h_out_ref,
                  ext_u, ext_c, gate_s, a_s, b_s, hs_s, hcar,
                  *, bb, seg, pos0, width):
    rows = bb * seg
    tt = pl.program_id(1)
    w = width
    gdim = w // len(POOL_WINDOWS)
    hdim = w // LRU_HEADS

    @pl.when(tt == 0)
    def _():
        ext_u[:, 0:POOL_HIST, :] = pool0_ref[...]
        ext_c[:, 0:CONV_HIST, :] = conv0_ref[...]
        hcar[...] = h0_ref[...].reshape(bb * SUBLANES, w)

    x = x_ref[...].reshape(rows, w)
    h = _rmsnorm(x, norm1_ref[...]) * (1.0 + sc_ref[0]) + sh_ref[0]
    hb = h.astype(BF16)

    row = lax.broadcasted_iota(I32, (rows, 1), 0)
    pos = pos0 + tt * seg + (row & (seg - 1))

    u = _dot(hb, w_in_ref[:, 0:w])
    ext_u[:, POOL_HIST:POOL_HIST + seg, :] = u.reshape(bb, seg, w)
    for g, win in enumerate(POOL_WINDOWS):
        lo = g * gdim
        acc = u[:, lo:lo + gdim]
        for j in range(1, win):
            acc = acc + ext_u[:, POOL_HIST - j:POOL_HIST - j + seg, lo:lo + gdim].reshape(rows, gdim)
        cnt = jnp.minimum(pos + 1, win).astype(F32)
        pooled = acc / cnt - u[:, lo:lo + gdim]
        zg = _dot(pooled.astype(BF16), pool_mix_ref[g])
        gate_s[:, lo:lo + gdim] = zg
    z = gate_s[...] * pool_scale_ref[...]
    y_a = _dot(z.astype(BF16), w_a_ref[...])
    pool_out_ref[...] = ext_u[:, seg:seg + POOL_HIST, :]
    ext_u[:, 0:POOL_HIST, :] = ext_u[:, seg:seg + POOL_HIST, :]

    gl = _dot(hb, w_in_ref[:, 3 * w:5 * w]) + b_gate_ref[...]
    gates = jax.nn.sigmoid(gl)
    m = gates[:, 0:w] * y_a

    xb = _dot(hb, w_in_ref[:, w:2 * w])
    ext_c[:, CONV_HIST:CONV_HIST + seg, :] = xb.reshape(bb, seg, w)
    xc = conv_b_ref[...] + xb * conv_w_ref[CONV_WIDTH - 1:CONV_WIDTH, :]
    for k in range(CONV_WIDTH - 1):
        sh = CONV_WIDTH - 1 - k
        xc = xc + ext_c[:, CONV_HIST - sh:CONV_HIST - sh + seg, :].reshape(rows, w) * conv_w_ref[k:k + 1, :]
    conv_out_ref[...] = ext_c[:, seg:seg + CONV_HIST, :]
    ext_c[:, 0:CONV_HIST, :] = ext_c[:, seg:seg + CONV_HIST, :]

    xcb = xc.astype(BF16)
    lam = -lam_ref[...]
    softplus = jnp.maximum(lam, 0.0) + jnp.log1p(jnp.exp(-jnp.abs(lam)))
    for hh in range(LRU_HEADS):
        lo = hh * hdim
        ri = _dot(xcb[:, lo:lo + hdim], w_gi_ref[hh])
        r = jax.nn.sigmoid(ri[:, 0:hdim] + b_rg_ref[:, lo:lo + hdim])
        i = jax.nn.sigmoid(ri[:, hdim:2 * hdim] + b_ig_ref[:, lo:lo + hdim])
        log_a = (-LRU_C) * r * softplus[:, lo:lo + hdim]
        a = jnp.exp(log_a)
        mult = jnp.sqrt(-jnp.tanh(log_a) * (a * a + 1.0))
        mult = jnp.where(pos == 0, 1.0, mult)
        a_s[:, lo:lo + hdim] = a
        b_s[:, lo:lo + hdim] = mult * i * xc[:, lo:lo + hdim]

    a = a_s[...]
    b = b_s[...]
    rin = row & (SUBLANES - 1)
    for d in (1, 2, 4):
        a_sh = jnp.where(rin >= d, pltpu.roll(a, d, 0), 1.0)
        b_sh = jnp.where(rin >= d, pltpu.roll(b, d, 0), 0.0)
        b = a * b_sh + b
        a = a * a_sh
    if seg == SUBLANES:
        hs = a * hcar[...] + b
        h_out_ref[...] = hs.reshape(bb, SUBLANES, w)
    else:
        a_s[...] = a
        b_s[...] = b

        def tile_step(k, hp):
            r0 = pl.multiple_of(k * SUBLANES, SUBLANES)
            ht = a_s[pl.ds(r0, SUBLANES), :] * hp + b_s[pl.ds(r0, SUBLANES), :]
            hs_s[pl.ds(r0, SUBLANES), :] = ht
            return jnp.broadcast_to(ht[SUBLANES - 1:SUBLANES, :], (SUBLANES, w))

        hp = lax.fori_loop(0, rows // SUBLANES, tile_step, hcar[...])
        hcar[...] = hp
        h_out_ref[...] = hp.reshape(bb, SUBLANES, w)
        hs = hs_s[...]

    yb = _dot(hb, w_in_ref[:, 2 * w:3 * w])
    y_b = _dot((hs * _gelu(yb)).astype(BF16), w_b_ref[...])
    m = m + gates[:, w:2 * w] * y_b
    out = _dot(m.astype(BF16), w_out_ref[...])
    x1_ref[...] = (x + g1_ref[0] * out).reshape(bb, seg, w)


def _const_spec(shape):
    nd = len(shape)
    return pl.BlockSpec(shape, lambda *_: (0,) * nd, pipeline_mode=pl.Buffered(1))


def _mixer(x, sh, sc, g1, pool0, conv0, h0e, weights, *, bb, seg, pos0, per_token_mod):
    bsz, t, d = x.shape
    w = d
    rows = bb * seg
    nb, nt = bsz // bb, t // seg
    rm = sh.shape[1]

    def mod_map(b, tt):
        return ((b * nt + tt) if per_token_mod else b, 0, 0)

    mod_spec = pl.BlockSpec((1, rm, d), mod_map)
    wnames = ("norm1", "w_in", "b_gate", "pool_mix", "pool_scale", "conv_w", "conv_b",
              "w_gi", "b_rg", "b_ig", "lam", "w_a", "w_b", "w_out")
    wvals = [weights[k] for k in wnames]
    kern = functools.partial(_mixer_kernel, bb=bb, seg=seg, pos0=pos0, width=w)
    return pl.pallas_call(
        kern,
        grid=(nb, nt),
        in_specs=[
            pl.BlockSpec((bb, seg, d), lambda b, tt: (b, tt, 0)),
            mod_spec, mod_spec, mod_spec,
            pl.BlockSpec((bb, POOL_HIST, w), lambda b, tt: (b, 0, 0)),
            pl.BlockSpec((bb, CONV_HIST, w), lambda b, tt: (b, 0, 0)),
            pl.BlockSpec((bb, SUBLANES, w), lambda b, tt: (b, 0, 0)),
        ] + [_const_spec(v.shape) for v in wvals],
        out_specs=[
            pl.BlockSpec((bb, seg, d), lambda b, tt: (b, tt, 0)),
            pl.BlockSpec((bb, POOL_HIST, w), lambda b, tt: (b, 0, 0)),
            pl.BlockSpec((bb, CONV_HIST, w), lambda b, tt: (b, 0, 0)),
            pl.BlockSpec((bb, SUBLANES, w), lambda b, tt: (b, 0, 0)),
        ],
        out_shape=[
            jax.ShapeDtypeStruct((bsz, t, d), F32),
            jax.ShapeDtypeStruct((bsz, POOL_HIST, w), F32),
            jax.ShapeDtypeStruct((bsz, CONV_HIST, w), F32),
            jax.ShapeDtypeStruct((bsz, SUBLANES, w), F32),
        ],
        scratch_shapes=[
            pltpu.VMEM((bb, POOL_HIST + seg, w), F32),
            pltpu.VMEM((bb, CONV_HIST + seg, w), F32),
            pltpu.VMEM((rows, w), F32),
            pltpu.VMEM((rows, w), F32),
            pltpu.VMEM((rows, w), F32),
            pltpu.VMEM((rows, w), F32),
            pltpu.VMEM((bb * SUBLANES, w), F32),
        ],
        compiler_params=pltpu.CompilerParams(
            dimension_semantics=("arbitrary", "arbitrary"), vmem_limit_bytes=VMEM_LIMIT),
        name="mixer",
    )(x, sh, sc, g1, pool0, conv0, h0e, *wvals)


def _top16_rows(s, val_ref, idx_ref):
    nk = s.shape[0]
    iota = lax.broadcasted_iota(I32, s.shape, 0).astype(F32)
    for it in range(PEER_TOPK):
        mx = jnp.max(s, axis=0, keepdims=True)
        ix = jnp.min(jnp.where(s == mx, iota, float(nk)), axis=0, keepdims=True)
        val_ref[it:it + 1, :] = mx
        idx_ref[it:it + 1, :] = ix
        s = jnp.where(iota == ix, -jnp.inf, s)


def _route_kernel(x_ref, sh_ref, sc_ref, norm2_ref, wq_ref, keys_ref,
                  h2_ref, e_ref, g_ref, h2_s, q_s, *list_refs, ntok):
    hd = pl.program_id(1)
    lists = [list_refs[6 * j:6 * j + 6] for j in range(ntok // LANES)]

    @pl.when(hd == 0)
    def _():
        x = x_ref[...]
        h2 = _rmsnorm(x, norm2_ref[...]) * (1.0 + sc_ref[0]) + sh_ref[0]
        h2b = h2.astype(BF16)
        h2_s[...] = h2b
        h2_ref[...] = h2b

    q_s[...] = _dot(h2_s[...], wq_ref[...])
    k1 = keys_ref[0, 0]
    k2 = keys_ref[0, 1]
    half = PEER_NKEYS

    riota = lax.broadcasted_iota(I32, (SUBLANES, LANES), 0).astype(F32)
    nkf = float(PEER_NKEYS)
    topf = float(PEER_TOPK)

    def sub_block(j):
        v1_s, i1_s, v2_s, i2_s, bs_s, be_s = lists[j]
        qj = q_s[j * LANES:(j + 1) * LANES, :].astype(BF16)
        _top16_rows(_dot_nt(k1, qj[:, 0:half]), v1_s, i1_s)
        _top16_rows(_dot_nt(k2, qj[:, half:2 * half]), v2_s, i2_s)
        s2_lo = v2_s[0:SUBLANES, :]
        s2_hi = v2_s[SUBLANES:2 * SUBLANES, :]
        i2_lo = i2_s[0:SUBLANES, :]
        i2_hi = i2_s[SUBLANES:2 * SUBLANES, :]
        cands, eids, flats = [], [], []
        v1a = v1_s[0:1, :]
        e1a = i1_s[0:1, :] * nkf
        cands += [v1a + s2_lo, v1a + s2_hi]
        eids += [e1a + i2_lo, e1a + i2_hi]
        flats += [riota, riota + float(SUBLANES)]
        for a in range(1, SUBLANES):
            nb = PEER_TOPK // (a + 1)
            c = v1_s[a:a + 1, :] + s2_lo
            if nb < SUBLANES:
                c = jnp.where(riota < float(nb), c, -jnp.inf)
            cands.append(c)
            eids.append(i1_s[a:a + 1, :] * nkf + i2_lo)
            flats.append(riota + a * topf)
        cands.append(v1_s[SUBLANES:2 * SUBLANES, :] + v2_s[0:1, :])
        eids.append(i1_s[SUBLANES:2 * SUBLANES, :] * nkf + i2_s[0:1, :])
        flats.append((riota + float(SUBLANES)) * topf)
        cand = jnp.concatenate(cands, axis=0)
        eid = jnp.concatenate(eids, axis=0)
        flat = jnp.concatenate(flats, axis=0)
        for it in range(PEER_TOPK):
            mx = jnp.max(cand, axis=0, keepdims=True)
            fx = jnp.min(jnp.where(cand == mx, flat, topf * topf), axis=0, keepdims=True)
            hit = flat == fx
            be_s[it:it + 1, :] = jnp.max(jnp.where(hit, eid, -1.0), axis=0, keepdims=True)
            bs_s[it:it + 1, :] = mx
            cand = jnp.where(hit, -jnp.inf, cand)
        best = bs_s[...]
        ex = jnp.exp(best - best[0:1, :])
        gsm = ex / jnp.sum(ex, axis=0, keepdims=True)
        e_ref[j] = be_s[...]
        g_ref[j] = gsm

    for j in range(ntok // LANES):
        sub_block(j)


def _route(x1, sh2, sc2, norm2, w_q, keys, *, tok_per_mod):
    n, d = x1.shape
    nt = ROUTE_TOK
    tiles = n // nt
    rm = sh2.shape[1]
    tiles_per_mod = tok_per_mod // nt
    mod_spec = pl.BlockSpec((1, rm, d), lambda i, hd: (i // tiles_per_mod, 0, 0))
    dk2 = w_q.shape[1] // PEER_HEADS
    kern = functools.partial(_route_kernel, ntok=nt)
    return pl.pallas_call(
        kern,
        grid=(tiles, PEER_HEADS),
        in_specs=[
            pl.BlockSpec((nt, d), lambda i, hd: (i, 0)),
            mod_spec, mod_spec,
            pl.BlockSpec((1, d), lambda i, hd: (0, 0)),
            pl.BlockSpec((d, dk2), lambda i, hd: (0, hd)),
            pl.BlockSpec((1, 2, PEER_NKEYS, dk2 // 2), lambda i, hd: (hd, 0, 0, 0)),
        ],
        out_specs=[
            pl.BlockSpec((nt, d), lambda i, hd: (i, 0)),
            pl.BlockSpec((nt // LANES, PEER_TOPK, LANES), lambda i, hd: (i, hd, 0)),
            pl.BlockSpec((nt // LANES, PEER_TOPK, LANES), lambda i, hd: (i, hd, 0)),
        ],
        out_shape=[
            jax.ShapeDtypeStruct((n, d), BF16),
            jax.ShapeDtypeStruct((n // LANES, PEER_HEADS * PEER_TOPK, LANES), F32),
            jax.ShapeDtypeStruct((n // LANES, PEER_HEADS * PEER_TOPK, LANES), F32),
        ],
        scratch_shapes=[
            pltpu.VMEM((nt, d), BF16),
            pltpu.VMEM((nt, dk2), F32),
        ] + [pltpu.VMEM((PEER_TOPK, LANES), F32)] * (6 * (nt // LANES)),
        compiler_params=pltpu.CompilerParams(
            dimension_semantics=("arbitrary", "arbitrary"), vmem_limit_bytes=VMEM_LIMIT),
        name="route",
    )(x1, sh2, sc2, norm2, w_q, keys)


def _expert_kernel(h2_ref, e_ref, g_ref, u_ref, v_ref, x1_ref, g2_ref, fnorm_ref,
                   y_ref, s_s, erow_s, grow_s, w2_s, acc_s, *, ntok):
    c = pl.program_id(1)
    nchunks = pl.num_programs(1)
    nk = PEER_NKEYS

    @pl.when(c == 0)
    def _():
        acc_s[...] = jnp.zeros_like(acc_s)
        for sb in range(ntok // LANES):
            erow_s[sb * LANES:(sb + 1) * LANES, :] = e_ref[sb].T.astype(I32)
            grow_s[sb * LANES:(sb + 1) * LANES, :] = g_ref[sb].T
        riota = lax.broadcasted_iota(I32, (nk, nk), 0)
        zeros = jnp.zeros((nk, nk), BF16)

        def pair(n0):
            lts, rts = [], []
            for t in range(2):
                e = erow_s[pl.ds(n0 + t, 1), :]
                gv = grow_s[pl.ds(n0 + t, 1), :]
                i1 = e >> 7
                i2 = e & (nk - 1)
                lts.append(jnp.where(riota == i1, gv, 0.0).astype(BF16))
                rts.append(jnp.where(riota == i2, 1.0, 0.0).astype(BF16))
            lhs = jnp.concatenate(lts, axis=1)
            rhs_t = jnp.concatenate([jnp.concatenate([rts[0], zeros], axis=1),
                                     jnp.concatenate([zeros, rts[1]], axis=1)], axis=0)
            gp = _dot_nt(lhs, rhs_t)
            for t in range(2):
                base = pl.multiple_of((n0 + t) * S_PITCH, SUBLANES)
                s_s[pl.ds(base, nk), :] = gp[:, t * nk:(t + 1) * nk]

        def pair_group(p, carry):
            for q in range(PAIR_UNROLL):
                pair(pl.multiple_of(2 * (p * PAIR_UNROLL + q), 2))
            return carry

        lax.fori_loop(0, ntok // (2 * PAIR_UNROLL), pair_group, 0)

    h2 = h2_ref[...]
    i1_per_sub = EXP_SUB // nk
    for j in range(EXP_CHUNK // EXP_SUB):
        a = _dot_nt(h2, u_ref[j * EXP_SUB:(j + 1) * EXP_SUB, :])
        i1_0 = c * (EXP_CHUNK // nk) + j * i1_per_sub
        gs = [s_s[pl.ds(i1_0 + t, ntok, stride=S_PITCH), :] for t in range(i1_per_sub)]
        g = jnp.concatenate(gs, axis=1)
        w2_s[:, j * EXP_SUB:(j + 1) * EXP_SUB] = (_gelu(a) * g).astype(BF16)
    acc_s[...] += _dot(w2_s[...], v_ref[...])

    @pl.when(c == nchunks - 1)
    def _():
        x2 = x1_ref[...] + g2_ref[0] * acc_s[...]
        y_ref[...] = _rmsnorm(x2, fnorm_ref[...])


def _expert(h2, eT, gT, u_bf, v_bf, x1, g2, fnorm, *, tok_per_mod):
    n, d = x1.shape
    nt = EXP_TOK
    ne = u_bf.shape[0]
    rm = g2.shape[1]
    blocks_per_mod = tok_per_mod // nt
    kern = functools.partial(_expert_kernel, ntok=nt)
    return pl.pallas_call(
        kern,
        grid=(n // nt, ne // EXP_CHUNK),
        in_specs=[
            pl.BlockSpec((nt, d), lambda i, c: (i, 0)),
            pl.BlockSpec((nt // LANES, PEER_HEADS * PEER_TOPK, LANES), lambda i, c: (i, 0, 0)),
            pl.BlockSpec((nt // LANES, PEER_HEADS * PEER_TOPK, LANES), lambda i, c: (i, 0, 0)),
            pl.BlockSpec((EXP_CHUNK, d), lambda i, c: (c, 0)),
            pl.BlockSpec((EXP_CHUNK, d), lambda i, c: (c, 0)),
            pl.BlockSpec((nt, d), lambda i, c: (i, 0)),
            pl.BlockSpec((1, rm, d), lambda i, c: (i // blocks_per_mod, 0, 0)),
            pl.BlockSpec((1, d), lambda i, c: (0, 0)),
        ],
        out_specs=pl.BlockSpec((nt, d), lambda i, c: (i, 0)),
        out_shape=jax.ShapeDtypeStruct((n, d), F32),
        scratch_shapes=[
            pltpu.VMEM((nt * S_PITCH, LANES), F32),
            pltpu.VMEM((nt, PEER_HEADS * PEER_TOPK), I32),
            pltpu.VMEM((nt, PEER_HEADS * PEER_TOPK), F32),
            pltpu.VMEM((nt, EXP_CHUNK), BF16),
            pltpu.VMEM((nt, d), F32),
        ],
        compiler_params=pltpu.CompilerParams(
            dimension_semantics=("arbitrary", "arbitrary"), vmem_limit_bytes=VMEM_LIMIT),
        name="expert",
    )(h2, eT, gT, u_bf, v_bf, x1, g2, fnorm)


def _run_group(x, mod, pool0, conv0, h0, wts, *, bb, seg, pos0, per_token_mod):
    bsz, t, d = x.shape
    sh1, sc1, g1, sh2, sc2, g2 = jnp.split(mod, 6, axis=-1)
    n = bsz * t

    def expand(m, tile_rows):
        if per_token_mod:
            return jnp.broadcast_to(m[:, None, :], (bsz, t, d)).reshape(n // tile_rows, tile_rows, d)
        return m[:, None, :]

    pool0p = jnp.pad(pool0, ((0, 0), (POOL_HIST - pool0.shape[1], 0), (0, 0)))
    conv0p = jnp.pad(conv0, ((0, 0), (CONV_HIST - conv0.shape[1], 0), (0, 0)))
    h0e = jnp.broadcast_to(h0[:, None, :], (bsz, SUBLANES, h0.shape[-1]))
    rows = bb * seg
    x1, pool_o, conv_o, h_o = _mixer(
        x, expand(sh1, rows), expand(sc1, rows), expand(g1, rows), pool0p, conv0p, h0e, wts,
        bb=bb, seg=seg, pos0=pos0, per_token_mod=per_token_mod)
    x1f = x1.reshape(n, d)
    tok_per_mod_r = ROUTE_TOK if per_token_mod else t
    h2, eT, gT = _route(x1f, expand(sh2, ROUTE_TOK), expand(sc2, ROUTE_TOK), wts["norm2"],
                        wts["w_q"], wts["keys"], tok_per_mod=tok_per_mod_r)
    tok_per_mod_e = EXP_TOK if per_token_mod else t
    y = _expert(h2, eT, gT, wts["u"], wts["v"], x1f, expand(g2, EXP_TOK), wts["fnorm"],
                tok_per_mod=tok_per_mod_e)
    return (y.reshape(bsz, t, d), pool_o[:, 1:], conv_o[:, CONV_HIST - (CONV_WIDTH - 1):],
            h_o[:, SUBLANES - 1])


def kernel(x_prompt, x_sample, c_prompt, c_sample, state_pool, state_conv, state_lru, norm1, norm2, w_ada, b_ada, w_in, b_gate, pool_mix, pool_scale, conv_w, conv_b, w_rg, b_rg, w_ig, b_ig, lru_lambda, w_branch_a, w_branch_b, w_out, w_q, sub_keys, expert_u, expert_v, final_norm):
    depth = w_in.shape[0]
    assert depth == 1, "single-layer trunk"
    bp, tp, d = x_prompt.shape
    bs, ts, _ = x_sample.shape
    l = 0
    mod = _ada(jnp.concatenate([c_prompt, c_sample], axis=0), w_ada[l], b_ada[l])
    row = lambda v: v.reshape(1, -1)
    wts = {
        "norm1": row(norm1[l]), "norm2": row(norm2[l]), "fnorm": row(final_norm),
        "w_in": w_in[l].astype(BF16), "b_gate": row(b_gate[l]),
        "pool_mix": pool_mix[l].astype(BF16), "pool_scale": row(pool_scale[l]),
        "conv_w": conv_w[l], "conv_b": row(conv_b[l]),
        "w_gi": jnp.concatenate([w_rg[l], w_ig[l]], axis=-1).astype(BF16),
        "b_rg": row(b_rg[l]), "b_ig": row(b_ig[l]), "lam": row(lru_lambda[l]),
        "w_a": w_branch_a[l].astype(BF16), "w_b": w_branch_b[l].astype(BF16),
        "w_out": w_out[l].astype(BF16), "w_q": w_q[l].astype(BF16),
        "keys": sub_keys[l].astype(BF16),
        "u": expert_u[l].astype(BF16), "v": expert_v[l].astype(BF16),
    }
    zp = jnp.zeros((bp, state_pool.shape[2], state_pool.shape[3]), F32)
    zc = jnp.zeros((bp, state_conv.shape[2], state_conv.shape[3]), F32)
    zh = jnp.zeros((bp, state_lru.shape[2]), F32)
    y_p, pool_p, conv_p, lru_p = _run_group(
        x_prompt, mod[:bp], zp, zc, zh, wts, bb=1, seg=256, pos0=0, per_token_mod=False)
    y_s, pool_s, conv_s, lru_s = _run_group(
        x_sample, mod[bp:], state_pool[l], state_conv[l], state_lru[l], wts,
        bb=32, seg=ts, pos0=PAST_LEN, per_token_mod=True)
    st = lambda v: v[None]
    return (y_p, y_s, st(pool_p), st(conv_p), st(lru_p).astype(state_lru.dtype),
            st(pool_s), st(conv_s), st(lru_s).astype(state_lru.dtype))
```

```python
import functools

import jax
import jax.numpy as jnp
from jax import lax
from jax.experimental import pallas as pl
from jax.experimental.pallas import tpu as pltpu

F32 = jnp.float32
BF16 = jnp.bfloat16
I32 = jnp.int32

NORM_EPS = 1e-6
PAST_LEN = 16384
LRU_C = 8.0
POOL_WINDOWS = (2, 4, 8, 16)
POOL_HIST = 16
CONV_WIDTH = 4
CONV_HIST = 8
LRU_HEADS = 8
PEER_HEADS = 8
PEER_NKEYS = 128
PEER_TOPK = 16
SUBLANES = 8
LANES = 128
VMEM_LIMIT = 56 * 1024 * 1024

EXP_TOK = 256
EXP_CHUNK = 2048
EXP_SUB = 256
PAIR_UNROLL = 32
S_PITCH = 136
ROUTE_TOK = 1024


def _dot(a, b):
    return jnp.dot(a, b, preferred_element_type=F32)


def _dot_nt(a, b):
    return lax.dot_general(a, b, (((1,), (1,)), ((), ())), preferred_element_type=F32)


def _rmsnorm(x, g):
    return x * lax.rsqrt(jnp.mean(x * x, axis=-1, keepdims=True) + NORM_EPS) * g


def _gelu(x):
    return jax.nn.gelu(x, approximate=True)


def _ada_kernel(c_ref, w_ref, b_ref, o_ref):
    c = c_ref[...]
    s = (c * jax.nn.sigmoid(c)).astype(BF16)
    o_ref[...] = _dot(s, w_ref[...].astype(BF16)) + b_ref[...]


def _ada(c, w_ada, b_ada):
    n, d = c.shape
    dout = w_ada.shape[1]
    tn = 1024
    return pl.pallas_call(
        _ada_kernel,
        grid=(dout // tn,),
        in_specs=[
            pl.BlockSpec((n, d), lambda j: (0, 0)),
            pl.BlockSpec((d, tn), lambda j: (0, j)),
            pl.BlockSpec((1, tn), lambda j: (0, j)),
        ],
        out_specs=pl.BlockSpec((n, tn), lambda j: (0, j)),
        out_shape=jax.ShapeDtypeStruct((n, dout), F32),
        compiler_params=pltpu.CompilerParams(
            dimension_semantics=("arbitrary",), vmem_limit_bytes=VMEM_LIMIT),
        name="ada",
    )(c, w_ada, b_ada.reshape(1, dout))


def _mixer_kernel(x_ref, sh_ref, sc_ref, g1_ref, pool0_ref, conv0_ref, h0_ref,
                  norm1_ref, w_in_ref, b_gate_ref, pool_mix_ref, pool_scale_ref,
                  conv_w_ref, conv_b_ref, w_gi_ref, b_rg_ref, b_ig_ref, lam_ref,
                  w_a_ref, w_b_ref, w_out_ref,
                  x1_ref, pool_out_ref, conv_out_ref, h_out_ref,
                  ext_u, ext_c, gate_s, a_s, b_s, hs_s, hcar,
                  *, bb, seg, pos0, width):
    rows = bb * seg
    tt = pl.program_id(1)
    w = width
    gdim = w // len(POOL_WINDOWS)
    hdim = w // LRU_HEADS

    @pl.when(tt == 0)
    def _():
        ext_u[:, 0:POOL_HIST, :] = pool0_ref[...]
        ext_c[:, 0:CONV_HIST, :] = conv0_ref[...]
        hcar[...] = h0_ref[...].reshape(bb * SUBLANES, w)

    x = x_ref[...].reshape(rows, w)
    h = _rmsnorm(x, norm1_ref[...]) * (1.0 + sc_ref[0]) + sh_ref[0]
    hb = h.astype(BF16)

    row = lax.broadcasted_iota(I32, (rows, 1), 0)
    pos = pos0 + tt * seg + (row & (seg - 1))

    u = _dot(hb, w_in_ref[:, 0:w])
    ext_u[:, POOL_HIST:POOL_HIST + seg, :] = u.reshape(bb, seg, w)
    for g, win in enumerate(POOL_WINDOWS):
        lo = g * gdim
        acc = u[:, lo:lo + gdim]
        for j in range(1, win):
            acc = acc + ext_u[:, POOL_HIST - j:POOL_HIST - j + seg, lo:lo + gdim].reshape(rows, gdim)
        cnt = jnp.minimum(pos + 1, win).astype(F32)
        pooled = acc / cnt - u[:, lo:lo + gdim]
        zg = _dot(pooled.astype(BF16), pool_mix_ref[g])
        gate_s[:, lo:lo + gdim] = zg
    z = gate_s[...] * pool_scale_ref[...]
    y_a = _dot(z.astype(BF16), w_a_ref[...])
    pool_out_ref[...] = ext_u[:, seg:seg + POOL_HIST, :]
    ext_u[:, 0:POOL_HIST, :] = ext_u[:, seg:seg + POOL_HIST, :]

    g_a = jax.nn.sigmoid(_dot(hb, w_in_ref[:, 3 * w:4 * w]) + b_gate_ref[:, 0:w])
    m = g_a * y_a

    xb = _dot(hb, w_in_ref[:, w:2 * w])
    ext_c[:, CONV_HIST:CONV_HIST + seg, :] = xb.reshape(bb, seg, w)
    xc = conv_b_ref[...] + xb * conv_w_ref[CONV_WIDTH - 1:CONV_WIDTH, :]
    for k in range(CONV_WIDTH - 1):
        sh = CONV_WIDTH - 1 - k
        xc = xc + ext_c[:, CONV_HIST - sh:CONV_HIST - sh + seg, :].reshape(rows, w) * conv_w_ref[k:k + 1, :]
    conv_out_ref[...] = ext_c[:, seg:seg + CONV_HIST, :]
    ext_c[:, 0:CONV_HIST, :] = ext_c[:, seg:seg + CONV_HIST, :]

    xcb = xc.astype(BF16)
    lam = -lam_ref[...]
    softplus = jnp.maximum(lam, 0.0) + jnp.log1p(jnp.exp(-jnp.abs(lam)))
    for hh in range(LRU_HEADS):
        lo = hh * hdim
        ri = _dot(xcb[:, lo:lo + hdim], w_gi_ref[hh])
        r = jax.nn.sigmoid(ri[:, 0:hdim] + b_rg_ref[:, lo:lo + hdim])
        i = jax.nn.sigmoid(ri[:, hdim:2 * hdim] + b_ig_ref[:, lo:lo + hdim])
        log_a = (-LRU_C) * r * softplus[:, lo:lo + hdim]
        a = jnp.exp(log_a)
        mult = jnp.sqrt(-jnp.tanh(log_a) * (a * a + 1.0))
        mult = jnp.where(pos == 0, 1.0, mult)
        a_s[:, lo:lo + hdim] = a
        b_s[:, lo:lo + hdim] = mult * i * xc[:, lo:lo + hdim]

    a = a_s[...]
    b = b_s[...]
    rin = row & (SUBLANES - 1)
    for d in (1, 2, 4):
        a_sh = jnp.where(rin >= d, pltpu.roll(a, d, 0), 1.0)
        b_sh = jnp.where(rin >= d, pltpu.roll(b, d, 0), 0.0)
        b = a * b_sh + b
        a = a * a_sh
    if seg == SUBLANES:
        hs = a * hcar[...] + b
        h_out_ref[...] = hs.reshape(bb, SUBLANES, w)
    else:
        a_s[...] = a
        b_s[...] = b

        def tile_step(k, hp):
            r0 = pl.multiple_of(k * SUBLANES, SUBLANES)
            ht = a_s[pl.ds(r0, SUBLANES), :] * hp + b_s[pl.ds(r0, SUBLANES), :]
            hs_s[pl.ds(r0, SUBLANES), :] = ht
            return jnp.broadcast_to(ht[SUBLANES - 1:SUBLANES, :], (SUBLANES, w))

        hp = lax.fori_loop(0, rows // SUBLANES, tile_step, hcar[...])
        hcar[...] = hp
        h_out_ref[...] = hp.reshape(bb, SUBLANES, w)
        hs = hs_s[...]

    yb = _dot(hb, w_in_ref[:, 2 * w:3 * w])
    y_b = _dot((hs * _gelu(yb)).astype(BF16), w_b_ref[...])
    g_b = jax.nn.sigmoid(_dot(hb, w_in_ref[:, 4 * w:5 * w]) + b_gate_ref[:, w:2 * w])
    m = m + g_b * y_b
    out = _dot(m.astype(BF16), w_out_ref[...])
    x1_ref[...] = (x + g1_ref[0] * out).reshape(bb, seg, w)


def _const_spec(shape):
    nd = len(shape)
    return pl.BlockSpec(shape, lambda *_: (0,) * nd, pipeline_mode=pl.Buffered(1))


def _mixer(x, sh, sc, g1, pool0, conv0, h0e, weights, *, bb, seg, pos0, per_token_mod):
    bsz, t, d = x.shape
    w = d
    rows = bb * seg
    nb, nt = bsz // bb, t // seg
    rm = sh.shape[1]

    def mod_map(b, tt):
        return ((b * nt + tt) if per_token_mod else b, 0, 0)

    mod_spec = pl.BlockSpec((1, rm, d), mod_map)
    wnames = ("norm1", "w_in", "b_gate", "pool_mix", "pool_scale", "conv_w", "conv_b",
              "w_gi", "b_rg", "b_ig", "lam", "w_a", "w_b", "w_out")
    wvals = [weights[k] for k in wnames]
    kern = functools.partial(_mixer_kernel, bb=bb, seg=seg, pos0=pos0, width=w)
    return pl.pallas_call(
        kern,
        grid=(nb, nt),
        in_specs=[
            pl.BlockSpec((bb, seg, d), lambda b, tt: (b, tt, 0)),
            mod_spec, mod_spec, mod_spec,
            pl.BlockSpec((bb, POOL_HIST, w), lambda b, tt: (b, 0, 0)),
            pl.BlockSpec((bb, CONV_HIST, w), lambda b, tt: (b, 0, 0)),
            pl.BlockSpec((bb, SUBLANES, w), lambda b, tt: (b, 0, 0)),
        ] + [_const_spec(v.shape) for v in wvals],
        out_specs=[
            pl.BlockSpec((bb, seg, d), lambda b, tt: (b, tt, 0)),
            pl.BlockSpec((bb, POOL_HIST, w), lambda b, tt: (b, 0, 0)),
            pl.BlockSpec((bb, CONV_HIST, w), lambda b, tt: (b, 0, 0)),
            pl.BlockSpec((bb, SUBLANES, w), lambda b, tt: (b, 0, 0)),
        ],
        out_shape=[
            jax.ShapeDtypeStruct((bsz, t, d), F32),
            jax.ShapeDtypeStruct((bsz, POOL_HIST, w), F32),
            jax.ShapeDtypeStruct((bsz, CONV_HIST, w), F32),
            jax.ShapeDtypeStruct((bsz, SUBLANES, w), F32),
        ],
        scratch_shapes=[
            pltpu.VMEM((bb, POOL_HIST + seg, w), F32),
            pltpu.VMEM((bb, CONV_HIST + seg, w), F32),
            pltpu.VMEM((rows, w), F32),
            pltpu.VMEM((rows, w), F32),
            pltpu.VMEM((rows, w), F32),
            pltpu.VMEM((rows, w), F32),
            pltpu.VMEM((bb * SUBLANES, w), F32),
        ],
        compiler_params=pltpu.CompilerParams(
            dimension_semantics=("arbitrary", "arbitrary"), vmem_limit_bytes=VMEM_LIMIT),
        name="mixer",
    )(x, sh, sc, g1, pool0, conv0, h0e, *wvals)


def _top16_rows(s, val_ref, idx_ref):
    nk = s.shape[0]
    iota = lax.broadcasted_iota(I32, s.shape, 0).astype(F32)
    for it in range(PEER_TOPK):
        mx = jnp.max(s, axis=0, keepdims=True)
        ix = jnp.min(jnp.where(s == mx, iota, float(nk)), axis=0, keepdims=True)
        val_ref[it:it + 1, :] = mx
        idx_ref[it:it + 1, :] = ix
        s = jnp.where(iota == ix, -jnp.inf, s)


def _route_kernel(x_ref, sh_ref, sc_ref, norm2_ref, wq_ref, keys_ref,
                  h2_ref, e_ref, g_ref, h2_s, q_s, *list_refs, ntok):
    hd = pl.program_id(1)
    lists = [list_refs[6 * j:6 * j + 6] for j in range(ntok // LANES)]

    @pl.when(hd == 0)
    def _():
        x = x_ref[...]
        h2 = _rmsnorm(x, norm2_ref[...]) * (1.0 + sc_ref[0]) + sh_ref[0]
        h2b = h2.astype(BF16)
        h2_s[...] = h2b
        h2_ref[...] = h2b

    q_s[...] = _dot(h2_s[...], wq_ref[...])
    k1 = keys_ref[0, 0]
    k2 = keys_ref[0, 1]
    half = PEER_NKEYS

    riota = lax.broadcasted_iota(I32, (SUBLANES, LANES), 0).astype(F32)
    nkf = float(PEER_NKEYS)
    topf = float(PEER_TOPK)
    nexp = nkf * nkf

    def sub_block(j):
        v1_s, i1_s, v2_s, i2_s, bs_s, bk_s = lists[j]
        qj = q_s[j * LANES:(j + 1) * LANES, :].astype(BF16)
        _top16_rows(_dot_nt(k1, qj[:, 0:half]), v1_s, i1_s)
        _top16_rows(_dot_nt(k2, qj[:, half:2 * half]), v2_s, i2_s)
        s2_lo = v2_s[0:SUBLANES, :]
        s2_hi = v2_s[SUBLANES:2 * SUBLANES, :]
        i2_lo = i2_s[0:SUBLANES, :]
        i2_hi = i2_s[SUBLANES:2 * SUBLANES, :]
        cands, keys = [], []
        v1a = v1_s[0:1, :]
        e1a = i1_s[0:1, :] * nkf
        cands += [v1a + s2_lo, v1a + s2_hi]
        keys += [riota * nexp + (e1a + i2_lo), (riota + float(SUBLANES)) * nexp + (e1a + i2_hi)]
        for a in range(1, SUBLANES):
            nb = PEER_TOPK // (a + 1)
            c = v1_s[a:a + 1, :] + s2_lo
            if nb < SUBLANES:
                c = jnp.where(riota < float(nb), c, -jnp.inf)
            cands.append(c)
            keys.append((riota + a * topf) * nexp + (i1_s[a:a + 1, :] * nkf + i2_lo))
        cands.append(v1_s[SUBLANES:2 * SUBLANES, :] + v2_s[0:1, :])
        keys.append((riota + float(SUBLANES)) * (topf * nexp)
                    + (i1_s[SUBLANES:2 * SUBLANES, :] * nkf + i2_s[0:1, :]))
        cand = jnp.concatenate(cands, axis=0)
        key = jnp.concatenate(keys, axis=0)
        for it in range(PEER_TOPK):
            mx = jnp.max(cand, axis=0, keepdims=True)
            kx = jnp.min(jnp.where(cand == mx, key, topf * topf * nexp), axis=0, keepdims=True)
            bk_s[it:it + 1, :] = kx
            bs_s[it:it + 1, :] = mx
            cand = jnp.where(key == kx, -jnp.inf, cand)
        best = bs_s[...]
        ex = jnp.exp(best - best[0:1, :])
        gsm = ex / jnp.sum(ex, axis=0, keepdims=True)
        bk = bk_s[...]
        e_ref[j] = bk - jnp.floor(bk / nexp) * nexp
        g_ref[j] = gsm

    for j in range(ntok // LANES):
        sub_block(j)


def _route(x1, sh2, sc2, norm2, w_q, keys, *, tok_per_mod):
    n, d = x1.shape
    nt = ROUTE_TOK
    tiles = n // nt
    rm = sh2.shape[1]
    tiles_per_mod = tok_per_mod // nt
    mod_spec = pl.BlockSpec((1, rm, d), lambda i, hd: (i // tiles_per_mod, 0, 0))
    dk2 = w_q.shape[1] // PEER_HEADS
    kern = functools.partial(_route_kernel, ntok=nt)
    return pl.pallas_call(
        kern,
        grid=(tiles, PEER_HEADS),
        in_specs=[
            pl.BlockSpec((nt, d), lambda i, hd: (i, 0)),
            mod_spec, mod_spec,
            pl.BlockSpec((1, d), lambda i, hd: (0, 0)),
            pl.BlockSpec((d, dk2), lambda i, hd: (0, hd)),
            pl.BlockSpec((1, 2, PEER_NKEYS, dk2 // 2), lambda i, hd: (hd, 0, 0, 0)),
        ],
        out_specs=[
            pl.BlockSpec((nt, d), lambda i, hd: (i, 0)),
            pl.BlockSpec((nt // LANES, PEER_TOPK, LANES), lambda i, hd: (i, hd, 0)),
            pl.BlockSpec((nt // LANES, PEER_TOPK, LANES), lambda i, hd: (i, hd, 0)),
        ],
        out_shape=[
            jax.ShapeDtypeStruct((n, d), BF16),
            jax.ShapeDtypeStruct((n // LANES, PEER_HEADS * PEER_TOPK, LANES), F32),
            jax.ShapeDtypeStruct((n // LANES, PEER_HEADS * PEER_TOPK, LANES), F32),
        ],
        scratch_shapes=[
            pltpu.VMEM((nt, d), BF16),
            pltpu.VMEM((nt, dk2), F32),
        ] + [pltpu.VMEM((PEER_TOPK, LANES), F32)] * (6 * (nt // LANES)),
        compiler_params=pltpu.CompilerParams(
            dimension_semantics=("arbitrary", "arbitrary"), vmem_limit_bytes=VMEM_LIMIT),
        name="route",
    )(x1, sh2, sc2, norm2, w_q, keys)


def _expert_kernel(h2_ref, e_ref, g_ref, u_ref, v_ref, x1_ref, g2_ref, fnorm_ref,
                   y_ref, s_s, erow_s, grow_s, w2_s, acc_s, *, ntok):
    c = pl.program_id(1)
    nchunks = pl.num_programs(1)
    nk = PEER_NKEYS

    @pl.when(c == 0)
    def _():
        acc_s[...] = jnp.zeros_like(acc_s)
        for sb in range(ntok // LANES):
            erow_s[sb * LANES:(sb + 1) * LANES, :] = e_ref[sb].T.astype(I32)
            grow_s[sb * LANES:(sb + 1) * LANES, :] = g_ref[sb].T
        riota = lax.broadcasted_iota(I32, (nk, nk), 0)
        zeros = jnp.zeros((nk, nk), BF16)

        def pair(n0):
            lts, rts = [], []
            for t in range(2):
                e = erow_s[pl.ds(n0 + t, 1), :]
                gv = grow_s[pl.ds(n0 + t, 1), :]
                i1 = e >> 7
                i2 = e & (nk - 1)
                lts.append(jnp.where(riota == i1, gv, 0.0).astype(BF16))
                rts.append(jnp.where(riota == i2, 1.0, 0.0).astype(BF16))
            lhs = jnp.concatenate(lts, axis=1)
            rhs_t = jnp.concatenate([jnp.concatenate([rts[0], zeros], axis=1),
                                     jnp.concatenate([zeros, rts[1]], axis=1)], axis=0)
            gp = _dot_nt(lhs, rhs_t)
            for t in range(2):
                base = pl.multiple_of((n0 + t) * S_PITCH, SUBLANES)
                s_s[pl.ds(base, nk), :] = gp[:, t * nk:(t + 1) * nk]

        def pair_group(p, carry):
            for q in range(PAIR_UNROLL):
                pair(pl.multiple_of(2 * (p * PAIR_UNROLL + q), 2))
            return carry

        lax.fori_loop(0, ntok // (2 * PAIR_UNROLL), pair_group, 0)

    h2 = h2_ref[...]
    i1_per_sub = EXP_SUB // nk
    for j in range(EXP_CHUNK // EXP_SUB):
        a = _dot_nt(h2, u_ref[j * EXP_SUB:(j + 1) * EXP_SUB, :])
        i1_0 = c * (EXP_CHUNK // nk) + j * i1_per_sub
        gs = [s_s[pl.ds(i1_0 + t, ntok, stride=S_PITCH), :] for t in range(i1_per_sub)]
        g = jnp.concatenate(gs, axis=1)
        w2_s[:, j * EXP_SUB:(j + 1) * EXP_SUB] = (_gelu(a) * g).astype(BF16)
    acc_s[...] += _dot(w2_s[...], v_ref[...])

    @pl.when(c == nchunks - 1)
    def _():
        x2 = x1_ref[...] + g2_ref[0] * acc_s[...]
        y_ref[...] = _rmsnorm(x2, fnorm_ref[...])


def _expert(h2, eT, gT, u_bf, v_bf, x1, g2, fnorm, *, tok_per_mod):
    n, d = x1.shape
    nt = EXP_TOK
    ne = u_bf.shape[0]
    rm = g2.shape[1]
    blocks_per_mod = tok_per_mod // nt
    kern = functools.partial(_expert_kernel, ntok=nt)
    return pl.pallas_call(
        kern,
        grid=(n // nt, ne // EXP_CHUNK),
        in_specs=[
            pl.BlockSpec((nt, d), lambda i, c: (i, 0)),
            pl.BlockSpec((nt // LANES, PEER_HEADS * PEER_TOPK, LANES), lambda i, c: (i, 0, 0)),
            pl.BlockSpec((nt // LANES, PEER_HEADS * PEER_TOPK, LANES), lambda i, c: (i, 0, 0)),
            pl.BlockSpec((EXP_CHUNK, d), lambda i, c: (c, 0)),
            pl.BlockSpec((EXP_CHUNK, d), lambda i, c: (c, 0)),
            pl.BlockSpec((nt, d), lambda i, c: (i, 0)),
            pl.BlockSpec((1, rm, d), lambda i, c: (i // blocks_per_mod, 0, 0)),
            pl.BlockSpec((1, d), lambda i, c: (0, 0)),
        ],
        out_specs=pl.BlockSpec((nt, d), lambda i, c: (i, 0)),
        out_shape=jax.ShapeDtypeStruct((n, d), F32),
        scratch_shapes=[
            pltpu.VMEM((nt * S_PITCH, LANES), F32),
            pltpu.VMEM((nt, PEER_HEADS * PEER_TOPK), I32),
            pltpu.VMEM((nt, PEER_HEADS * PEER_TOPK), F32),
            pltpu.VMEM((nt, EXP_CHUNK), BF16),
            pltpu.VMEM((nt, d), F32),
        ],
        compiler_params=pltpu.CompilerParams(
            dimension_semantics=("arbitrary", "arbitrary"), vmem_limit_bytes=VMEM_LIMIT),
        name="expert",
    )(h2, eT, gT, u_bf, v_bf, x1, g2, fnorm)


def _run_group(x, mod, pool0, conv0, h0, wts, *, bb, seg, pos0, per_token_mod):
    bsz, t, d = x.shape
    sh1, sc1, g1, sh2, sc2, g2 = jnp.split(mod, 6, axis=-1)
    n = bsz * t

    def expand(m, tile_rows):
        if per_token_mod:
            return jnp.broadcast_to(m[:, None, :], (bsz, t, d)).reshape(n // tile_rows, tile_rows, d)
        return m[:, None, :]

    pool0p = jnp.pad(pool0, ((0, 0), (POOL_HIST - pool0.shape[1], 0), (0, 0)))
    conv0p = jnp.pad(conv0, ((0, 0), (CONV_HIST - conv0.shape[1], 0), (0, 0)))
    h0e = jnp.broadcast_to(h0[:, None, :], (bsz, SUBLANES, h0.shape[-1]))
    rows = bb * seg
    x1, pool_o, conv_o, h_o = _mixer(
        x, expand(sh1, rows), expand(sc1, rows), expand(g1, rows), pool0p, conv0p, h0e, wts,
        bb=bb, seg=seg, pos0=pos0, per_token_mod=per_token_mod)
    x1f = x1.reshape(n, d)
    tok_per_mod_r = ROUTE_TOK if per_token_mod else t
    h2, eT, gT = _route(x1f, expand(sh2, ROUTE_TOK), expand(sc2, ROUTE_TOK), wts["norm2"],
                        wts["w_q"], wts["keys"], tok_per_mod=tok_per_mod_r)
    tok_per_mod_e = EXP_TOK if per_token_mod else t
    y = _expert(h2, eT, gT, wts["u"], wts["v"], x1f, expand(g2, EXP_TOK), wts["fnorm"],
                tok_per_mod=tok_per_mod_e)
    return (y.reshape(bsz, t, d), pool_o[:, 1:], conv_o[:, CONV_HIST - (CONV_WIDTH - 1):],
            h_o[:, SUBLANES - 1])


def kernel(x_prompt, x_sample, c_prompt, c_sample, state_pool, state_conv, state_lru, norm1, norm2, w_ada, b_ada, w_in, b_gate, pool_mix, pool_scale, conv_w, conv_b, w_rg, b_rg, w_ig, b_ig, lru_lambda, w_branch_a, w_branch_b, w_out, w_q, sub_keys, expert_u, expert_v, final_norm):
    depth = w_in.shape[0]
    assert depth == 1, "single-layer trunk"
    bp, tp, d = x_prompt.shape
    bs, ts, _ = x_sample.shape
    l = 0
    mod = _ada(jnp.concatenate([c_prompt, c_sample], axis=0), w_ada[l], b_ada[l])
    row = lambda v: v.reshape(1, -1)
    wts = {
        "norm1": row(norm1[l]), "norm2": row(norm2[l]), "fnorm": row(final_norm),
        "w_in": w_in[l].astype(BF16), "b_gate": row(b_gate[l]),
        "pool_mix": pool_mix[l].astype(BF16), "pool_scale": row(pool_scale[l]),
        "conv_w": conv_w[l], "conv_b": row(conv_b[l]),
        "w_gi": jnp.concatenate([w_rg[l], w_ig[l]], axis=-1).astype(BF16),
        "b_rg": row(b_rg[l]), "b_ig": row(b_ig[l]), "lam": row(lru_lambda[l]),
        "w_a": w_branch_a[l].astype(BF16), "w_b": w_branch_b[l].astype(BF16),
        "w_out": w_out[l].astype(BF16), "w_q": w_q[l].astype(BF16),
        "keys": sub_keys[l].astype(BF16),
        "u": expert_u[l].astype(BF16), "v": expert_v[l].astype(BF16),
    }
    zp = jnp.zeros((bp, state_pool.shape[2], state_pool.shape[3]), F32)
    zc = jnp.zeros((bp, state_conv.shape[2], state_conv.shape[3]), F32)
    zh = jnp.zeros((bp, state_lru.shape[2]), F32)
    y_p, pool_p, conv_p, lru_p = _run_group(
        x_prompt, mod[:bp], zp, zc, zh, wts, bb=1, seg=256, pos0=0, per_token_mod=False)
    y_s, pool_s, conv_s, lru_s = _run_group(
        x_sample, mod[bp:], state_pool[l], state_conv[l], state_lru[l], wts,
        bb=32, seg=ts, pos0=PAST_LEN, per_token_mod=True)
    st = lambda v: v[None]
    return (y_p, y_s, st(pool_p), st(conv_p), st(lru_p).astype(state_lru.dtype),
            st(pool_s), st(conv_s), st(lru_s).astype(state_lru.dtype))
```

```python
import functools

import jax
import jax.numpy as jnp
from jax import lax
from jax.experimental import pallas as pl
from jax.experimental.pallas import tpu as pltpu

F32 = jnp.float32
BF16 = jnp.bfloat16
I32 = jnp.int32

NORM_EPS = 1e-6
PAST_LEN = 16384
LRU_C = 8.0
POOL_WINDOWS = (2, 4, 8, 16)
POOL_HIST = 16
CONV_WIDTH = 4
CONV_HIST = 8
LRU_HEADS = 8
PEER_HEADS = 8
PEER_NKEYS = 128
PEER_TOPK = 16
SUBLANES = 8
LANES = 128
VMEM_LIMIT = 56 * 1024 * 1024

EXP_TOK = 512
EXP_CHUNK = 2048
EXP_SUB = 256
PAIR_UNROLL = 32
S_PITCH = 136
ROUTE_TOK = 1024


def _dot(a, b):
    return jnp.dot(a, b, preferred_element_type=F32)


def _dot_nt(a, b):
    return lax.dot_general(a, b, (((1,), (1,)), ((), ())), preferred_element_type=F32)


def _rmsnorm(x, g):
    return x * lax.rsqrt(jnp.mean(x * x, axis=-1, keepdims=True) + NORM_EPS) * g


def _gelu(x):
    return jax.nn.gelu(x, approximate=True)


def _ada_kernel(c_ref, w_ref, b_ref, o_ref):
    c = c_ref[...]
    s = (c * jax.nn.sigmoid(c)).astype(BF16)
    o_ref[...] = _dot(s, w_ref[...].astype(BF16)) + b_ref[...]


def _ada(c, w_ada, b_ada):
    n, d = c.shape
    dout = w_ada.shape[1]
    tn = 1024
    return pl.pallas_call(
        _ada_kernel,
        grid=(dout // tn,),
        in_specs=[
            pl.BlockSpec((n, d), lambda j: (0, 0)),
            pl.BlockSpec((d, tn), lambda j: (0, j)),
            pl.BlockSpec((1, tn), lambda j: (0, j)),
        ],
        out_specs=pl.BlockSpec((n, tn), lambda j: (0, j)),
        out_shape=jax.ShapeDtypeStruct((n, dout), F32),
        compiler_params=pltpu.CompilerParams(
            dimension_semantics=("arbitrary",), vmem_limit_bytes=VMEM_LIMIT),
        name="ada",
    )(c, w_ada, b_ada.reshape(1, dout))


def _mixer_kernel(x_ref, sh_ref, sc_ref, g1_ref, pool0_ref, conv0_ref, h0_ref,
                  norm1_ref, w_in_ref, b_gate_ref, pool_mix_ref, pool_scale_ref,
                  conv_w_ref, conv_b_ref, w_gi_ref, b_rg_ref, b_ig_ref, lam_ref,
                  w_a_ref, w_b_ref, w_out_ref,
                  x1_ref, pool_out_ref, conv_out_ref, h_out_ref,
                  ext_u, ext_c, gate_s, a_s, b_s, hs_s, hcar,
                  *, bb, seg, pos0, width):
    rows = bb * seg
    tt = pl.program_id(1)
    w = width
    gdim = w // len(POOL_WINDOWS)
    hdim = w // LRU_HEADS

    @pl.when(tt == 0)
    def _():
        ext_u[:, 0:POOL_HIST, :] = pool0_ref[...]
        ext_c[:, 0:CONV_HIST, :] = conv0_ref[...]
        hcar[...] = h0_ref[...].reshape(bb * SUBLANES, w)

    x = x_ref[...].reshape(rows, w)
    h = _rmsnorm(x, norm1_ref[...]) * (1.0 + sc_ref[0]) + sh_ref[0]
    hb = h.astype(BF16)

    row = lax.broadcasted_iota(I32, (rows, 1), 0)
    pos = pos0 + tt * seg + (row & (seg - 1))

    u = _dot(hb, w_in_ref[:, 0:w])
    ext_u[:, POOL_HIST:POOL_HIST + seg, :] = u.reshape(bb, seg, w)
    for g, win in enumerate(POOL_WINDOWS):
        lo = g * gdim
        acc = u[:, lo:lo + gdim]
        for j in range(1, win):
            acc = acc + ext_u[:, POOL_HIST - j:POOL_HIST - j + seg, lo:lo + gdim].reshape(rows, gdim)
        cnt = jnp.minimum(pos + 1, win).astype(F32)
        pooled = acc / cnt - u[:, lo:lo + gdim]
        zg = _dot(pooled.astype(BF16), pool_mix_ref[g])
        gate_s[:, lo:lo + gdim] = zg
    z = gate_s[...] * pool_scale_ref[...]
    y_a = _dot(z.astype(BF16), w_a_ref[...])
    pool_out_ref[...] = ext_u[:, seg:seg + POOL_HIST, :]
    ext_u[:, 0:POOL_HIST, :] = ext_u[:, seg:seg + POOL_HIST, :]

    g_a = jax.nn.sigmoid(_dot(hb, w_in_ref[:, 3 * w:4 * w]) + b_gate_ref[:, 0:w])
    m = g_a * y_a

    xb = _dot(hb, w_in_ref[:, w:2 * w])
    ext_c[:, CONV_HIST:CONV_HIST + seg, :] = xb.reshape(bb, seg, w)
    xc = conv_b_ref[...] + xb * conv_w_ref[CONV_WIDTH - 1:CONV_WIDTH, :]
    for k in range(CONV_WIDTH - 1):
        sh = CONV_WIDTH - 1 - k
        xc = xc + ext_c[:, CONV_HIST - sh:CONV_HIST - sh + seg, :].reshape(rows, w) * conv_w_ref[k:k + 1, :]
    conv_out_ref[...] = ext_c[:, seg:seg + CONV_HIST, :]
    ext_c[:, 0:CONV_HIST, :] = ext_c[:, seg:seg + CONV_HIST, :]

    xcb = xc.astype(BF16)
    lam = -lam_ref[...]
    softplus = jnp.maximum(lam, 0.0) + jnp.log1p(jnp.exp(-jnp.abs(lam)))
    for hh in range(LRU_HEADS):
        lo = hh * hdim
        ri = _dot(xcb[:, lo:lo + hdim], w_gi_ref[hh])
        r = jax.nn.sigmoid(ri[:, 0:hdim] + b_rg_ref[:, lo:lo + hdim])
        i = jax.nn.sigmoid(ri[:, hdim:2 * hdim] + b_ig_ref[:, lo:lo + hdim])
        log_a = (-LRU_C) * r * softplus[:, lo:lo + hdim]
        a = jnp.exp(log_a)
        mult = jnp.sqrt(-jnp.tanh(log_a) * (a * a + 1.0))
        mult = jnp.where(pos == 0, 1.0, mult)
        a_s[:, lo:lo + hdim] = a
        b_s[:, lo:lo + hdim] = mult * i * xc[:, lo:lo + hdim]

    a = a_s[...]
    b = b_s[...]
    rin = row & (SUBLANES - 1)
    for d in (1, 2, 4):
        a_sh = jnp.where(rin >= d, pltpu.roll(a, d, 0), 1.0)
        b_sh = jnp.where(rin >= d, pltpu.roll(b, d, 0), 0.0)
        b = a * b_sh + b
        a = a * a_sh
    if seg == SUBLANES:
        hs = a * hcar[...] + b
        h_out_ref[...] = hs.reshape(bb, SUBLANES, w)
    else:
        a_s[...] = a
        b_s[...] = b

        def tile_step(k, hp):
            r0 = pl.multiple_of(k * SUBLANES, SUBLANES)
            ht = a_s[pl.ds(r0, SUBLANES), :] * hp + b_s[pl.ds(r0, SUBLANES), :]
            hs_s[pl.ds(r0, SUBLANES), :] = ht
            return jnp.broadcast_to(ht[SUBLANES - 1:SUBLANES, :], (SUBLANES, w))

        hp = lax.fori_loop(0, rows // SUBLANES, tile_step, hcar[...])
        hcar[...] = hp
        h_out_ref[...] = hp.reshape(bb, SUBLANES, w)
        hs = hs_s[...]

    yb = _dot(hb, w_in_ref[:, 2 * w:3 * w])
    y_b = _dot((hs * _gelu(yb)).astype(BF16), w_b_ref[...])
    g_b = jax.nn.sigmoid(_dot(hb, w_in_ref[:, 4 * w:5 * w]) + b_gate_ref[:, w:2 * w])
    m = m + g_b * y_b
    out = _dot(m.astype(BF16), w_out_ref[...])
    x1_ref[...] = (x + g1_ref[0] * out).reshape(bb, seg, w)


def _const_spec(shape):
    nd = len(shape)
    return pl.BlockSpec(shape, lambda *_: (0,) * nd, pipeline_mode=pl.Buffered(1))


def _mixer(x, sh, sc, g1, pool0, conv0, h0e, weights, *, bb, seg, pos0, per_token_mod):
    bsz, t, d = x.shape
    w = d
    rows = bb * seg
    nb, nt = bsz // bb, t // seg
    rm = sh.shape[1]

    def mod_map(b, tt):
        return ((b * nt + tt) if per_token_mod else b, 0, 0)

    mod_spec = pl.BlockSpec((1, rm, d), mod_map)
    wnames = ("norm1", "w_in", "b_gate", "pool_mix", "pool_scale", "conv_w", "conv_b",
              "w_gi", "b_rg", "b_ig", "lam", "w_a", "w_b", "w_out")
    wvals = [weights[k] for k in wnames]
    kern = functools.partial(_mixer_kernel, bb=bb, seg=seg, pos0=pos0, width=w)
    return pl.pallas_call(
        kern,
        grid=(nb, nt),
        in_specs=[
            pl.BlockSpec((bb, seg, d), lambda b, tt: (b, tt, 0)),
            mod_spec, mod_spec, mod_spec,
            pl.BlockSpec((bb, POOL_HIST, w), lambda b, tt: (b, 0, 0)),
            pl.BlockSpec((bb, CONV_HIST, w), lambda b, tt: (b, 0, 0)),
            pl.BlockSpec((bb, SUBLANES, w), lambda b, tt: (b, 0, 0)),
        ] + [_const_spec(v.shape) for v in wvals],
        out_specs=[
            pl.BlockSpec((bb, seg, d), lambda b, tt: (b, tt, 0)),
            pl.BlockSpec((bb, POOL_HIST, w), lambda b, tt: (b, 0, 0)),
            pl.BlockSpec((bb, CONV_HIST, w), lambda b, tt: (b, 0, 0)),
            pl.BlockSpec((bb, SUBLANES, w), lambda b, tt: (b, 0, 0)),
        ],
        out_shape=[
            jax.ShapeDtypeStruct((bsz, t, d), F32),
            jax.ShapeDtypeStruct((bsz, POOL_HIST, w), F32),
            jax.ShapeDtypeStruct((bsz, CONV_HIST, w), F32),
            jax.ShapeDtypeStruct((bsz, SUBLANES, w), F32),
        ],
        scratch_shapes=[
            pltpu.VMEM((bb, POOL_HIST + seg, w), F32),
            pltpu.VMEM((bb, CONV_HIST + seg, w), F32),
            pltpu.VMEM((rows, w), F32),
            pltpu.VMEM((rows, w), F32),
            pltpu.VMEM((rows, w), F32),
            pltpu.VMEM((rows, w), F32),
            pltpu.VMEM((bb * SUBLANES, w), F32),
        ],
        compiler_params=pltpu.CompilerParams(
            dimension_semantics=("arbitrary", "arbitrary"), vmem_limit_bytes=VMEM_LIMIT),
        name="mixer",
    )(x, sh, sc, g1, pool0, conv0, h0e, *wvals)


def _top16_rows(s, val_ref, idx_ref):
    nk = s.shape[0]
    iota = lax.broadcasted_iota(I32, s.shape, 0).astype(F32)
    for it in range(PEER_TOPK):
        mx = jnp.max(s, axis=0, keepdims=True)
        ix = jnp.min(jnp.where(s == mx, iota, float(nk)), axis=0, keepdims=True)
        val_ref[it:it + 1, :] = mx
        idx_ref[it:it + 1, :] = ix
        s = jnp.where(iota == ix, -jnp.inf, s)


def _route_kernel(x_ref, sh_ref, sc_ref, norm2_ref, wq_ref, keys_ref,
                  h2_ref, e_ref, g_ref, h2_s, q_s, *list_refs, ntok):
    hd = pl.program_id(1)
    lists = [list_refs[6 * j:6 * j + 6] for j in range(ntok // LANES)]

    @pl.when(hd == 0)
    def _():
        x = x_ref[...]
        h2 = _rmsnorm(x, norm2_ref[...]) * (1.0 + sc_ref[0]) + sh_ref[0]
        h2b = h2.astype(BF16)
        h2_s[...] = h2b
        h2_ref[...] = h2b

    q_s[...] = _dot(h2_s[...], wq_ref[...])
    k1 = keys_ref[0, 0]
    k2 = keys_ref[0, 1]
    half = PEER_NKEYS

    riota = lax.broadcasted_iota(I32, (SUBLANES, LANES), 0).astype(F32)
    nkf = float(PEER_NKEYS)
    topf = float(PEER_TOPK)
    nexp = nkf * nkf

    def sub_block(j):
        v1_s, i1_s, v2_s, i2_s, bs_s, bk_s = lists[j]
        qj = q_s[j * LANES:(j + 1) * LANES, :].astype(BF16)
        _top16_rows(_dot_nt(k1, qj[:, 0:half]), v1_s, i1_s)
        _top16_rows(_dot_nt(k2, qj[:, half:2 * half]), v2_s, i2_s)
        s2_lo = v2_s[0:SUBLANES, :]
        s2_hi = v2_s[SUBLANES:2 * SUBLANES, :]
        i2_lo = i2_s[0:SUBLANES, :]
        i2_hi = i2_s[SUBLANES:2 * SUBLANES, :]
        cands, keys = [], []
        v1a = v1_s[0:1, :]
        e1a = i1_s[0:1, :] * nkf
        cands += [v1a + s2_lo, v1a + s2_hi]
        keys += [riota * nexp + (e1a + i2_lo), (riota + float(SUBLANES)) * nexp + (e1a + i2_hi)]
        for a in range(1, SUBLANES):
            nb = PEER_TOPK // (a + 1)
            c = v1_s[a:a + 1, :] + s2_lo
            if nb < SUBLANES:
                c = jnp.where(riota < float(nb), c, -jnp.inf)
            cands.append(c)
            keys.append((riota + a * topf) * nexp + (i1_s[a:a + 1, :] * nkf + i2_lo))
        cands.append(v1_s[SUBLANES:2 * SUBLANES, :] + v2_s[0:1, :])
        keys.append((riota + float(SUBLANES)) * (topf * nexp)
                    + (i1_s[SUBLANES:2 * SUBLANES, :] * nkf + i2_s[0:1, :]))
        cand = jnp.concatenate(cands, axis=0)
        key = jnp.concatenate(keys, axis=0)
        for it in range(PEER_TOPK):
            mx = jnp.max(cand, axis=0, keepdims=True)
            kx = jnp.min(jnp.where(cand == mx, key, topf * topf * nexp), axis=0, keepdims=True)
            bk_s[it:it + 1, :] = kx
            bs_s[it:it + 1, :] = mx
            cand = jnp.where(key == kx, -jnp.inf, cand)
        best = bs_s[...]
        ex = jnp.exp(best - best[0:1, :])
        gsm = ex / jnp.sum(ex, axis=0, keepdims=True)
        bk = bk_s[...]
        e_ref[j] = bk - jnp.floor(bk / nexp) * nexp
        g_ref[j] = gsm

    for j in range(ntok // LANES):
        sub_block(j)


def _route(x1, sh2, sc2, norm2, w_q, keys, *, tok_per_mod):
    n, d = x1.shape
    nt = ROUTE_TOK
    tiles = n // nt
    rm = sh2.shape[1]
    tiles_per_mod = tok_per_mod // nt
    mod_spec = pl.BlockSpec((1, rm, d), lambda i, hd: (i // tiles_per_mod, 0, 0))
    dk2 = w_q.shape[1] // PEER_HEADS
    kern = functools.partial(_route_kernel, ntok=nt)
    return pl.pallas_call(
        kern,
        grid=(tiles, PEER_HEADS),
        in_specs=[
            pl.BlockSpec((nt, d), lambda i, hd: (i, 0)),
            mod_spec, mod_spec,
            pl.BlockSpec((1, d), lambda i, hd: (0, 0)),
            pl.BlockSpec((d, dk2), lambda i, hd: (0, hd)),
            pl.BlockSpec((1, 2, PEER_NKEYS, dk2 // 2), lambda i, hd: (hd, 0, 0, 0)),
        ],
        out_specs=[
            pl.BlockSpec((nt, d), lambda i, hd: (i, 0)),
            pl.BlockSpec((nt // LANES, PEER_TOPK, LANES), lambda i, hd: (i, hd, 0)),
            pl.BlockSpec((nt // LANES, PEER_TOPK, LANES), lambda i, hd: (i, hd, 0)),
        ],
        out_shape=[
            jax.ShapeDtypeStruct((n, d), BF16),
            jax.ShapeDtypeStruct((n // LANES, PEER_HEADS * PEER_TOPK, LANES), F32),
            jax.ShapeDtypeStruct((n // LANES, PEER_HEADS * PEER_TOPK, LANES), F32),
        ],
        scratch_shapes=[
            pltpu.VMEM((nt, d), BF16),
            pltpu.VMEM((nt, dk2), F32),
        ] + [pltpu.VMEM((PEER_TOPK, LANES), F32)] * (6 * (nt // LANES)),
        compiler_params=pltpu.CompilerParams(
            dimension_semantics=("arbitrary", "arbitrary"), vmem_limit_bytes=VMEM_LIMIT),
        name="route",
    )(x1, sh2, sc2, norm2, w_q, keys)


def _expert_kernel(h2_ref, e_ref, g_ref, u_ref, v_ref, x1_ref, g2_ref, fnorm_ref,
                   y_ref, s_s, erow_s, grow_s, w2_s, acc_s, *, ntok):
    c = pl.program_id(1)
    nchunks = pl.num_programs(1)
    nk = PEER_NKEYS

    @pl.when(c == 0)
    def _():
        acc_s[...] = jnp.zeros_like(acc_s)
        for sb in range(ntok // LANES):
            erow_s[sb * LANES:(sb + 1) * LANES, :] = e_ref[sb].T.astype(I32)
            grow_s[sb * LANES:(sb + 1) * LANES, :] = g_ref[sb].T
        riota = lax.broadcasted_iota(I32, (nk, nk), 0)
        riota2 = lax.broadcasted_iota(I32, (2 * nk, nk), 0)

        def pair(p):
            n0 = pl.multiple_of(2 * p, 2)
            lts, rts = [], []
            for t in range(2):
                e = erow_s[pl.ds(n0 + t, 1), :]
                gv = grow_s[pl.ds(n0 + t, 1), :]
                i1 = e >> 7
                i2 = e & (nk - 1)
                lts.append(jnp.where(riota2 == 2 * i1 + t, gv, 0.0).astype(BF16))
                rts.append(jnp.where(riota == i2, 1.0, 0.0).astype(BF16))
            x = _dot_nt(jnp.concatenate(lts, axis=1), jnp.concatenate(rts, axis=1))
            words = pltpu.bitcast(x.astype(BF16), I32)
            s_s[pl.ds(pl.multiple_of(p * S_PITCH, SUBLANES), nk), :] = words

        def pair_group(pg, carry):
            for q in range(PAIR_UNROLL):
                pair(pg * PAIR_UNROLL + q)
            return carry

        lax.fori_loop(0, ntok // (2 * PAIR_UNROLL), pair_group, 0)

    h2 = h2_ref[...]
    i1_per_sub = EXP_SUB // nk
    for j in range(EXP_CHUNK // EXP_SUB):
        a = _dot_nt(h2, u_ref[j * EXP_SUB:(j + 1) * EXP_SUB, :])
        i1_0 = c * (EXP_CHUNK // nk) + j * i1_per_sub
        gs = [pltpu.bitcast(s_s[pl.ds(i1_0 + t, ntok // 2, stride=S_PITCH), :], BF16)
              for t in range(i1_per_sub)]
        g = jnp.concatenate(gs, axis=1).astype(F32)
        w2_s[:, j * EXP_SUB:(j + 1) * EXP_SUB] = (_gelu(a) * g).astype(BF16)
    acc_s[...] += _dot(w2_s[...], v_ref[...])

    @pl.when(c == nchunks - 1)
    def _():
        x2 = x1_ref[...] + g2_ref[0] * acc_s[...]
        y_ref[...] = _rmsnorm(x2, fnorm_ref[...])


def _expert(h2, eT, gT, u_bf, v_bf, x1, g2, fnorm, *, tok_per_mod):
    n, d = x1.shape
    nt = EXP_TOK
    ne = u_bf.shape[0]
    rm = g2.shape[1]
    blocks_per_mod = tok_per_mod // nt
    kern = functools.partial(_expert_kernel, ntok=nt)
    return pl.pallas_call(
        kern,
        grid=(n // nt, ne // EXP_CHUNK),
        in_specs=[
            pl.BlockSpec((nt, d), lambda i, c: (i, 0)),
            pl.BlockSpec((nt // LANES, PEER_HEADS * PEER_TOPK, LANES), lambda i, c: (i, 0, 0)),
            pl.BlockSpec((nt // LANES, PEER_HEADS * PEER_TOPK, LANES), lambda i, c: (i, 0, 0)),
            pl.BlockSpec((EXP_CHUNK, d), lambda i, c: (c, 0)),
            pl.BlockSpec((EXP_CHUNK, d), lambda i, c: (c, 0)),
            pl.BlockSpec((nt, d), lambda i, c: (i, 0)),
            pl.BlockSpec((1, rm, d), lambda i, c: (i // blocks_per_mod, 0, 0)),
            pl.BlockSpec((1, d), lambda i, c: (0, 0)),
        ],
        out_specs=pl.BlockSpec((nt, d), lambda i, c: (i, 0)),
        out_shape=jax.ShapeDtypeStruct((n, d), F32),
        scratch_shapes=[
            pltpu.VMEM((nt // 2 * S_PITCH, LANES), I32),
            pltpu.VMEM((nt, PEER_HEADS * PEER_TOPK), I32),
            pltpu.VMEM((nt, PEER_HEADS * PEER_TOPK), F32),
            pltpu.VMEM((nt, EXP_CHUNK), BF16),
            pltpu.VMEM((nt, d), F32),
        ],
        compiler_params=pltpu.CompilerParams(
            dimension_semantics=("arbitrary", "arbitrary"), vmem_limit_bytes=VMEM_LIMIT),
        name="expert",
    )(h2, eT, gT, u_bf, v_bf, x1, g2, fnorm)


def _run_group(x, mod, pool0, conv0, h0, wts, *, bb, seg, pos0, per_token_mod):
    bsz, t, d = x.shape
    sh1, sc1, g1, sh2, sc2, g2 = jnp.split(mod, 6, axis=-1)
    n = bsz * t

    def expand(m, tile_rows):
        if per_token_mod:
            return jnp.broadcast_to(m[:, None, :], (bsz, t, d)).reshape(n // tile_rows, tile_rows, d)
        return m[:, None, :]

    pool0p = jnp.pad(pool0, ((0, 0), (POOL_HIST - pool0.shape[1], 0), (0, 0)))
    conv0p = jnp.pad(conv0, ((0, 0), (CONV_HIST - conv0.shape[1], 0), (0, 0)))
    h0e = jnp.broadcast_to(h0[:, None, :], (bsz, SUBLANES, h0.shape[-1]))
    rows = bb * seg
    x1, pool_o, conv_o, h_o = _mixer(
        x, expand(sh1, rows), expand(sc1, rows), expand(g1, rows), pool0p, conv0p, h0e, wts,
        bb=bb, seg=seg, pos0=pos0, per_token_mod=per_token_mod)
    x1f = x1.reshape(n, d)
    tok_per_mod_r = ROUTE_TOK if per_token_mod else t
    h2, eT, gT = _route(x1f, expand(sh2, ROUTE_TOK), expand(sc2, ROUTE_TOK), wts["norm2"],
                        wts["w_q"], wts["keys"], tok_per_mod=tok_per_mod_r)
    tok_per_mod_e = EXP_TOK if per_token_mod else t
    y = _expert(h2, eT, gT, wts["u"], wts["v"], x1f, expand(g2, EXP_TOK), wts["fnorm"],
                tok_per_mod=tok_per_mod_e)
    return (y.reshape(bsz, t, d), pool_o[:, 1:], conv_o[:, CONV_HIST - (CONV_WIDTH - 1):],
            h_o[:, SUBLANES - 1])


def kernel(x_prompt, x_sample, c_prompt, c_sample, state_pool, state_conv, state_lru, norm1, norm2, w_ada, b_ada, w_in, b_gate, pool_mix, pool_scale, conv_w, conv_b, w_rg, b_rg, w_ig, b_ig, lru_lambda, w_branch_a, w_branch_b, w_out, w_q, sub_keys, expert_u, expert_v, final_norm):
    depth = w_in.shape[0]
    assert depth == 1, "single-layer trunk"
    bp, tp, d = x_prompt.shape
    bs, ts, _ = x_sample.shape
    l = 0
    mod = _ada(jnp.concatenate([c_prompt, c_sample], axis=0), w_ada[l], b_ada[l])
    row = lambda v: v.reshape(1, -1)
    wts = {
        "norm1": row(norm1[l]), "norm2": row(norm2[l]), "fnorm": row(final_norm),
        "w_in": w_in[l].astype(BF16), "b_gate": row(b_gate[l]),
        "pool_mix": pool_mix[l].astype(BF16), "pool_scale": row(pool_scale[l]),
        "conv_w": conv_w[l], "conv_b": row(conv_b[l]),
        "w_gi": jnp.concatenate([w_rg[l], w_ig[l]], axis=-1).astype(BF16),
        "b_rg": row(b_rg[l]), "b_ig": row(b_ig[l]), "lam": row(lru_lambda[l]),
        "w_a": w_branch_a[l].astype(BF16), "w_b": w_branch_b[l].astype(BF16),
        "w_out": w_out[l].astype(BF16), "w_q": w_q[l].astype(BF16),
        "keys": sub_keys[l].astype(BF16),
        "u": expert_u[l].astype(BF16), "v": expert_v[l].astype(BF16),
    }
    zp = jnp.zeros((bp, state_pool.shape[2], state_pool.shape[3]), F32)
    zc = jnp.zeros((bp, state_conv.shape[2], state_conv.shape[3]), F32)
    zh = jnp.zeros((bp, state_lru.shape[2]), F32)
    y_p, pool_p, conv_p, lru_p = _run_group(
        x_prompt, mod[:bp], zp, zc, zh, wts, bb=1, seg=256, pos0=0, per_token_mod=False)
    y_s, pool_s, conv_s, lru_s = _run_group(
        x_sample, mod[bp:], state_pool[l], state_conv[l], state_lru[l], wts,
        bb=32, seg=ts, pos0=PAST_LEN, per_token_mod=True)
    st = lambda v: v[None]
    return (y_p, y_s, st(pool_p), st(conv_p), st(lru_p).astype(state_lru.dtype),
            st(pool_s), st(conv_s), st(lru_s).astype(state_lru.dtype))
```

```python
import functools

import jax
import jax.numpy as jnp
from jax import lax
from jax.experimental import pallas as pl
from jax.experimental.pallas import tpu as pltpu

F32 = jnp.float32
BF16 = jnp.bfloat16
I32 = jnp.int32

NORM_EPS = 1e-6
PAST_LEN = 16384
LRU_C = 8.0
POOL_WINDOWS = (2, 4, 8, 16)
POOL_HIST = 16
CONV_WIDTH = 4
CONV_HIST = 8
LRU_HEADS = 8
PEER_HEADS = 8
PEER_NKEYS = 128
PEER_TOPK = 16
SUBLANES = 8
LANES = 128
VMEM_LIMIT = 56 * 1024 * 1024

EXP_TOK = 512
EXP_CHUNK = 2048
EXP_SUB = 256
PAIR_UNROLL = 32
S_PITCH = 136
ROUTE_TOK = 1024


def _dot(a, b):
    return jnp.dot(a, b, preferred_element_type=F32)


def _dot_nt(a, b):
    return lax.dot_general(a, b, (((1,), (1,)), ((), ())), preferred_element_type=F32)


def _rmsnorm(x, g):
    return x * lax.rsqrt(jnp.mean(x * x, axis=-1, keepdims=True) + NORM_EPS) * g


def _gelu(x):
    return jax.nn.gelu(x, approximate=True)


def _ada_kernel(c_ref, w_ref, b_ref, o_ref):
    c = c_ref[...]
    s = (c * jax.nn.sigmoid(c)).astype(BF16)
    o_ref[...] = _dot(s, w_ref[...].astype(BF16)) + b_ref[...]


def _ada(c, w_ada, b_ada):
    n, d = c.shape
    dout = w_ada.shape[1]
    tn = 1024
    return pl.pallas_call(
        _ada_kernel,
        grid=(dout // tn,),
        in_specs=[
            pl.BlockSpec((n, d), lambda j: (0, 0)),
            pl.BlockSpec((d, tn), lambda j: (0, j)),
            pl.BlockSpec((1, tn), lambda j: (0, j)),
        ],
        out_specs=pl.BlockSpec((n, tn), lambda j: (0, j)),
        out_shape=jax.ShapeDtypeStruct((n, dout), F32),
        compiler_params=pltpu.CompilerParams(
            dimension_semantics=("arbitrary",), vmem_limit_bytes=VMEM_LIMIT),
        name="ada",
    )(c, w_ada, b_ada.reshape(1, dout))


def _mixer_kernel(x_ref, sh_ref, sc_ref, g1_ref, pool0_ref, conv0_ref, h0_ref,
                  norm1_ref, w_in_ref, b_gate_ref, pool_mix_ref, pool_scale_ref,
                  conv_w_ref, conv_b_ref, w_gi_ref, b_rg_ref, b_ig_ref, lam_ref,
                  w_a_ref, w_b_ref, w_out_ref,
                  x1_ref, pool_out_ref, conv_out_ref, h_out_ref,
                  ext_u, ext_c, gate_s, a_s, b_s, hs_s, hcar, proj_s,
                  *, bb, seg, pos0, width):
    rows = bb * seg
    tt = pl.program_id(1)
    w = width
    gdim = w // len(POOL_WINDOWS)
    hdim = w // LRU_HEADS

    @pl.when(tt == 0)
    def _():
        ext_u[:, 0:POOL_HIST, :] = pool0_ref[...]
        ext_c[:, 0:CONV_HIST, :] = conv0_ref[...]
        hcar[...] = h0_ref[...].reshape(bb * SUBLANES, w)

    x = x_ref[...].reshape(rows, w)
    h = _rmsnorm(x, norm1_ref[...]) * (1.0 + sc_ref[0]) + sh_ref[0]
    hb = h.astype(BF16)

    row = lax.broadcasted_iota(I32, (rows, 1), 0)
    pos = pos0 + tt * seg + (row & (seg - 1))

    proj_s[...] = _dot(hb, w_in_ref[...])
    u = proj_s[:, 0:w]
    ext_u[:, POOL_HIST:POOL_HIST + seg, :] = u.reshape(bb, seg, w)
    ext_rows = bb * (POOL_HIST + seg)
    wsum = ext_u[:, :, 0:w].reshape(ext_rows, w)
    level = 1
    for g, win in enumerate(POOL_WINDOWS):
        lo = g * gdim
        while level < win:
            wsum = wsum + pltpu.roll(wsum, level, 0)
            level *= 2
        acc = wsum[:, 0:gdim].reshape(bb, POOL_HIST + seg, gdim)[:, POOL_HIST:, :].reshape(rows, gdim)
        wsum = wsum[:, gdim:]
        cnt = jnp.minimum(pos + 1, win).astype(F32)
        pooled = acc / cnt - u[:, lo:lo + gdim]
        zg = _dot(pooled.astype(BF16), pool_mix_ref[g])
        gate_s[:, lo:lo + gdim] = zg
    z = gate_s[...] * pool_scale_ref[...]
    y_a = _dot(z.astype(BF16), w_a_ref[...])
    pool_out_ref[...] = ext_u[:, seg:seg + POOL_HIST, :]
    ext_u[:, 0:POOL_HIST, :] = ext_u[:, seg:seg + POOL_HIST, :]

    g_a = jax.nn.sigmoid(proj_s[:, 3 * w:4 * w] + b_gate_ref[:, 0:w])
    m = g_a * y_a

    xb = proj_s[:, w:2 * w]
    ext_c[:, CONV_HIST:CONV_HIST + seg, :] = xb.reshape(bb, seg, w)
    xc = conv_b_ref[...] + xb * conv_w_ref[CONV_WIDTH - 1:CONV_WIDTH, :]
    for k in range(CONV_WIDTH - 1):
        sh = CONV_WIDTH - 1 - k
        xc = xc + ext_c[:, CONV_HIST - sh:CONV_HIST - sh + seg, :].reshape(rows, w) * conv_w_ref[k:k + 1, :]
    conv_out_ref[...] = ext_c[:, seg:seg + CONV_HIST, :]
    ext_c[:, 0:CONV_HIST, :] = ext_c[:, seg:seg + CONV_HIST, :]

    xcb = xc.astype(BF16)
    lam = -lam_ref[...]
    softplus = jnp.maximum(lam, 0.0) + jnp.log1p(jnp.exp(-jnp.abs(lam)))
    for hh in range(LRU_HEADS):
        lo = hh * hdim
        ri = _dot(xcb[:, lo:lo + hdim], w_gi_ref[hh])
        r = jax.nn.sigmoid(ri[:, 0:hdim] + b_rg_ref[:, lo:lo + hdim])
        i = jax.nn.sigmoid(ri[:, hdim:2 * hdim] + b_ig_ref[:, lo:lo + hdim])
        log_a = (-LRU_C) * r * softplus[:, lo:lo + hdim]
        a = jnp.exp(log_a)
        mult = jnp.sqrt(-jnp.tanh(log_a) * (a * a + 1.0))
        mult = jnp.where(pos == 0, 1.0, mult)
        a_s[:, lo:lo + hdim] = a
        b_s[:, lo:lo + hdim] = mult * i * xc[:, lo:lo + hdim]

    tiles = rows // SUBLANES
    a = a_s[...].reshape(tiles, SUBLANES, w)
    b = b_s[...].reshape(tiles, SUBLANES, w)
    rin = lax.broadcasted_iota(I32, (1, SUBLANES, 1), 1)
    for d in (1, 2, 4):
        a_sh = jnp.where(rin >= d, pltpu.roll(a, d, 1), 1.0)
        b_sh = jnp.where(rin >= d, pltpu.roll(b, d, 1), 0.0)
        b = a * b_sh + b
        a = a * a_sh
    a = a.reshape(rows, w)
    b = b.reshape(rows, w)
    if seg == SUBLANES:
        hs = a * hcar[...] + b
        h_out_ref[...] = hs.reshape(bb, SUBLANES, w)
    else:
        a_s[...] = a
        b_s[...] = b

        def tile_step(k, hp):
            r0 = pl.multiple_of(k * SUBLANES, SUBLANES)
            ht = a_s[pl.ds(r0, SUBLANES), :] * hp + b_s[pl.ds(r0, SUBLANES), :]
            hs_s[pl.ds(r0, SUBLANES), :] = ht
            return jnp.broadcast_to(ht[SUBLANES - 1:SUBLANES, :], (SUBLANES, w))

        hp = lax.fori_loop(0, rows // SUBLANES, tile_step, hcar[...])
        hcar[...] = hp
        h_out_ref[...] = hp.reshape(bb, SUBLANES, w)
        hs = hs_s[...]

    yb = proj_s[:, 2 * w:3 * w]
    y_b = _dot((hs * _gelu(yb)).astype(BF16), w_b_ref[...])
    g_b = jax.nn.sigmoid(proj_s[:, 4 * w:5 * w] + b_gate_ref[:, w:2 * w])
    m = m + g_b * y_b
    out = _dot(m.astype(BF16), w_out_ref[...])
    x1_ref[...] = (x + g1_ref[0] * out).reshape(bb, seg, w)


def _const_spec(shape):
    nd = len(shape)
    return pl.BlockSpec(shape, lambda *_: (0,) * nd, pipeline_mode=pl.Buffered(1))


def _mixer(x, sh, sc, g1, pool0, conv0, h0e, weights, *, bb, seg, pos0, per_token_mod):
    bsz, t, d = x.shape
    w = d
    rows = bb * seg
    nb, nt = bsz // bb, t // seg
    rm = sh.shape[1]

    def mod_map(b, tt):
        return ((b * nt + tt) if per_token_mod else b, 0, 0)

    mod_spec = pl.BlockSpec((1, rm, d), mod_map)
    wnames = ("norm1", "w_in", "b_gate", "pool_mix", "pool_scale", "conv_w", "conv_b",
              "w_gi", "b_rg", "b_ig", "lam", "w_a", "w_b", "w_out")
    wvals = [weights[k] for k in wnames]
    kern = functools.partial(_mixer_kernel, bb=bb, seg=seg, pos0=pos0, width=w)
    return pl.pallas_call(
        kern,
        grid=(nb, nt),
        in_specs=[
            pl.BlockSpec((bb, seg, d), lambda b, tt: (b, tt, 0)),
            mod_spec, mod_spec, mod_spec,
            pl.BlockSpec((bb, POOL_HIST, w), lambda b, tt: (b, 0, 0)),
            pl.BlockSpec((bb, CONV_HIST, w), lambda b, tt: (b, 0, 0)),
            pl.BlockSpec((bb, SUBLANES, w), lambda b, tt: (b, 0, 0)),
        ] + [_const_spec(v.shape) for v in wvals],
        out_specs=[
            pl.BlockSpec((bb, seg, d), lambda b, tt: (b, tt, 0)),
            pl.BlockSpec((bb, POOL_HIST, w), lambda b, tt: (b, 0, 0)),
            pl.BlockSpec((bb, CONV_HIST, w), lambda b, tt: (b, 0, 0)),
            pl.BlockSpec((bb, SUBLANES, w), lambda b, tt: (b, 0, 0)),
        ],
        out_shape=[
            jax.ShapeDtypeStruct((bsz, t, d), F32),
            jax.ShapeDtypeStruct((bsz, POOL_HIST, w), F32),
            jax.ShapeDtypeStruct((bsz, CONV_HIST, w), F32),
            jax.ShapeDtypeStruct((bsz, SUBLANES, w), F32),
        ],
        scratch_shapes=[
            pltpu.VMEM((bb, POOL_HIST + seg, w), F32),
            pltpu.VMEM((bb, CONV_HIST + seg, w), F32),
            pltpu.VMEM((rows, w), F32),
            pltpu.VMEM((rows, w), F32),
            pltpu.VMEM((rows, w), F32),
            pltpu.VMEM((rows, w), F32),
            pltpu.VMEM((bb * SUBLANES, w), F32),
            pltpu.VMEM((rows, weights["w_in"].shape[1]), F32),
        ],
        compiler_params=pltpu.CompilerParams(
            dimension_semantics=("arbitrary", "arbitrary"), vmem_limit_bytes=VMEM_LIMIT),
        name="mixer",
    )(x, sh, sc, g1, pool0, conv0, h0e, *wvals)


def _sort16_network():
    n, pairs, p = 16, [], 1
    while p < n:
        k = p
        while k >= 1:
            for j in range(k % p, n - k, 2 * k):
                for i in range(min(k, n - j - k)):
                    if (i + j) // (2 * p) == (i + j + k) // (2 * p):
                        pairs.append((i + j, i + j + k))
            k //= 2
        p *= 2
    return pairs


def _top16_rows(s, val_ref, idx_ref):
    nt = s.shape[0] // SUBLANES
    assert nt == PEER_TOPK
    sub = lax.broadcasted_iota(I32, (SUBLANES, LANES), 0).astype(F32)
    v = [s[SUBLANES * k:SUBLANES * (k + 1), :] for k in range(nt)]
    ix = [sub + float(SUBLANES * k) for k in range(nt)]
    for i, j in _sort16_network():
        swap = (v[j] > v[i]) | ((v[j] == v[i]) & (ix[j] < ix[i]))
        v[i], v[j] = jnp.where(swap, v[j], v[i]), jnp.where(swap, v[i], v[j])
        ix[i], ix[j] = jnp.where(swap, ix[j], ix[i]), jnp.where(swap, ix[i], ix[j])
    for it in range(PEER_TOPK):
        mx = jnp.max(v[0], axis=0, keepdims=True)
        best = jnp.min(jnp.where(v[0] == mx, ix[0], float(s.shape[0])), axis=0, keepdims=True)
        val_ref[it:it + 1, :] = mx
        idx_ref[it:it + 1, :] = best
        win = ix[0] == best
        for r in range(PEER_TOPK - 1 - it):
            v[r] = jnp.where(win, v[r + 1], v[r])
            ix[r] = jnp.where(win, ix[r + 1], ix[r])


def _route_kernel(x_ref, sh_ref, sc_ref, norm2_ref, wq_ref, keys_ref,
                  h2_ref, e_ref, g_ref, h2_s, q_s, *list_refs, ntok):
    hd = pl.program_id(1)
    lists = [list_refs[6 * j:6 * j + 6] for j in range(ntok // LANES)]

    @pl.when(hd == 0)
    def _():
        x = x_ref[...]
        h2 = _rmsnorm(x, norm2_ref[...]) * (1.0 + sc_ref[0]) + sh_ref[0]
        h2b = h2.astype(BF16)
        h2_s[...] = h2b
        h2_ref[...] = h2b

    q_s[...] = _dot(h2_s[...], wq_ref[...])
    k1 = keys_ref[0, 0]
    k2 = keys_ref[0, 1]
    half = PEER_NKEYS

    riota = lax.broadcasted_iota(I32, (SUBLANES, LANES), 0).astype(F32)
    nkf = float(PEER_NKEYS)
    topf = float(PEER_TOPK)
    nexp = nkf * nkf

    def sub_block(j):
        v1_s, i1_s, v2_s, i2_s, bs_s, bk_s = lists[j]
        qj = q_s[j * LANES:(j + 1) * LANES, :].astype(BF16)
        _top16_rows(_dot_nt(k1, qj[:, 0:half]), v1_s, i1_s)
        _top16_rows(_dot_nt(k2, qj[:, half:2 * half]), v2_s, i2_s)
        s2_lo = v2_s[0:SUBLANES, :]
        s2_hi = v2_s[SUBLANES:2 * SUBLANES, :]
        i2_lo = i2_s[0:SUBLANES, :]
        i2_hi = i2_s[SUBLANES:2 * SUBLANES, :]
        cands, keys = [], []
        v1a = v1_s[0:1, :]
        e1a = i1_s[0:1, :] * nkf
        cands += [v1a + s2_lo, v1a + s2_hi]
        keys += [riota * nexp + (e1a + i2_lo), (riota + float(SUBLANES)) * nexp + (e1a + i2_hi)]
        for a in range(1, SUBLANES):
            nb = PEER_TOPK // (a + 1)
            c = v1_s[a:a + 1, :] + s2_lo
            if nb < SUBLANES:
                c = jnp.where(riota < float(nb), c, -jnp.inf)
            cands.append(c)
            keys.append((riota + a * topf) * nexp + (i1_s[a:a + 1, :] * nkf + i2_lo))
        cands.append(v1_s[SUBLANES:2 * SUBLANES, :] + v2_s[0:1, :])
        keys.append((riota + float(SUBLANES)) * (topf * nexp)
                    + (i1_s[SUBLANES:2 * SUBLANES, :] * nkf + i2_s[0:1, :]))
        cand = jnp.concatenate(cands, axis=0)
        key = jnp.concatenate(keys, axis=0)
        for it in range(PEER_TOPK):
            mx = jnp.max(cand, axis=0, keepdims=True)
            kx = jnp.min(jnp.where(cand == mx, key, topf * topf * nexp), axis=0, keepdims=True)
            bk_s[it:it + 1, :] = kx
            bs_s[it:it + 1, :] = mx
            if it + 1 < PEER_TOPK:
                cand = jnp.where(key == kx, -jnp.inf, cand)
        best = bs_s[...]
        ex = jnp.exp(best - best[0:1, :])
        gsm = ex / jnp.sum(ex, axis=0, keepdims=True)
        bk = bk_s[...]
        e_ref[j] = bk - jnp.floor(bk / nexp) * nexp
        g_ref[j] = gsm

    for j in range(ntok // LANES):
        sub_block(j)


def _route(x1, sh2, sc2, norm2, w_q, keys, *, tok_per_mod):
    n, d = x1.shape
    nt = ROUTE_TOK
    tiles = n // nt
    rm = sh2.shape[1]
    tiles_per_mod = tok_per_mod // nt
    mod_spec = pl.BlockSpec((1, rm, d), lambda i, hd: (i // tiles_per_mod, 0, 0))
    dk2 = w_q.shape[1] // PEER_HEADS
    kern = functools.partial(_route_kernel, ntok=nt)
    return pl.pallas_call(
        kern,
        grid=(tiles, PEER_HEADS),
        in_specs=[
            pl.BlockSpec((nt, d), lambda i, hd: (i, 0)),
            mod_spec, mod_spec,
            pl.BlockSpec((1, d), lambda i, hd: (0, 0)),
            pl.BlockSpec((d, dk2), lambda i, hd: (0, hd)),
            pl.BlockSpec((1, 2, PEER_NKEYS, dk2 // 2), lambda i, hd: (hd, 0, 0, 0)),
        ],
        out_specs=[
            pl.BlockSpec((nt, d), lambda i, hd: (i, 0)),
            pl.BlockSpec((nt // LANES, PEER_TOPK, LANES), lambda i, hd: (i, hd, 0)),
            pl.BlockSpec((nt // LANES, PEER_TOPK, LANES), lambda i, hd: (i, hd, 0)),
        ],
        out_shape=[
            jax.ShapeDtypeStruct((n, d), BF16),
            jax.ShapeDtypeStruct((n // LANES, PEER_HEADS * PEER_TOPK, LANES), F32),
            jax.ShapeDtypeStruct((n // LANES, PEER_HEADS * PEER_TOPK, LANES), F32),
        ],
        scratch_shapes=[
            pltpu.VMEM((nt, d), BF16),
            pltpu.VMEM((nt, dk2), F32),
        ] + [pltpu.VMEM((PEER_TOPK, LANES), F32)] * (6 * (nt // LANES)),
        compiler_params=pltpu.CompilerParams(
            dimension_semantics=("arbitrary", "arbitrary"), vmem_limit_bytes=VMEM_LIMIT),
        name="route",
    )(x1, sh2, sc2, norm2, w_q, keys)


def _expert_kernel(h2_ref, e_ref, g_ref, u_ref, v_ref, x1_ref, g2_ref, fnorm_ref,
                   y_ref, s_s, erow_s, grow_s, w2_s, acc_s, *, ntok):
    c = pl.program_id(1)
    nchunks = pl.num_programs(1)
    nk = PEER_NKEYS

    @pl.when(c == 0)
    def _():
        acc_s[...] = jnp.zeros_like(acc_s)
        for sb in range(ntok // LANES):
            erow_s[sb * LANES:(sb + 1) * LANES, :] = e_ref[sb].T.astype(I32)
            grow_s[sb * LANES:(sb + 1) * LANES, :] = g_ref[sb].T
        riota = lax.broadcasted_iota(I32, (nk, nk), 0)
        riota2 = lax.broadcasted_iota(I32, (2 * nk, nk), 0)

        def pair(p):
            n0 = pl.multiple_of(2 * p, 2)
            lts, rts = [], []
            for t in range(2):
                e = erow_s[pl.ds(n0 + t, 1), :]
                gv = grow_s[pl.ds(n0 + t, 1), :]
                i1 = e >> 7
                i2 = e & (nk - 1)
                lts.append(jnp.where(riota2 == 2 * i1 + t, gv, 0.0).astype(BF16))
                rts.append(jnp.where(riota == i2, 1.0, 0.0).astype(BF16))
            x = _dot_nt(jnp.concatenate(lts, axis=1), jnp.concatenate(rts, axis=1))
            words = pltpu.bitcast(x.astype(BF16), I32)
            s_s[pl.ds(pl.multiple_of(p * S_PITCH, SUBLANES), nk), :] = words

        def pair_group(pg, carry):
            for q in range(PAIR_UNROLL):
                pair(pg * PAIR_UNROLL + q)
            return carry

        lax.fori_loop(0, ntok // (2 * PAIR_UNROLL), pair_group, 0)

    h2 = h2_ref[...]
    i1_per_sub = EXP_SUB // nk
    for j in range(EXP_CHUNK // EXP_SUB):
        a = _dot_nt(h2, u_ref[j * EXP_SUB:(j + 1) * EXP_SUB, :])
        i1_0 = c * (EXP_CHUNK // nk) + j * i1_per_sub
        gs = [pltpu.bitcast(s_s[pl.ds(i1_0 + t, ntok // 2, stride=S_PITCH), :], BF16)
              for t in range(i1_per_sub)]
        g = jnp.concatenate(gs, axis=1).astype(F32)
        w2_s[:, j * EXP_SUB:(j + 1) * EXP_SUB] = (_gelu(a) * g).astype(BF16)
    acc_s[...] += _dot(w2_s[...], v_ref[...])

    @pl.when(c == nchunks - 1)
    def _():
        x2 = x1_ref[...] + g2_ref[0] * acc_s[...]
        y_ref[...] = _rmsnorm(x2, fnorm_ref[...])


def _expert(h2, eT, gT, u_bf, v_bf, x1, g2, fnorm, *, tok_per_mod):
    n, d = x1.shape
    nt = EXP_TOK
    ne = u_bf.shape[0]
    rm = g2.shape[1]
    blocks_per_mod = tok_per_mod // nt
    kern = functools.partial(_expert_kernel, ntok=nt)
    return pl.pallas_call(
        kern,
        grid=(n // nt, ne // EXP_CHUNK),
        in_specs=[
            pl.BlockSpec((nt, d), lambda i, c: (i, 0)),
            pl.BlockSpec((nt // LANES, PEER_HEADS * PEER_TOPK, LANES), lambda i, c: (i, 0, 0)),
            pl.BlockSpec((nt // LANES, PEER_HEADS * PEER_TOPK, LANES), lambda i, c: (i, 0, 0)),
            pl.BlockSpec((EXP_CHUNK, d), lambda i, c: (c, 0)),
            pl.BlockSpec((EXP_CHUNK, d), lambda i, c: (c, 0)),
            pl.BlockSpec((nt, d), lambda i, c: (i, 0)),
            pl.BlockSpec((1, rm, d), lambda i, c: (i // blocks_per_mod, 0, 0)),
            pl.BlockSpec((1, d), lambda i, c: (0, 0)),
        ],
        out_specs=pl.BlockSpec((nt, d), lambda i, c: (i, 0)),
        out_shape=jax.ShapeDtypeStruct((n, d), F32),
        scratch_shapes=[
            pltpu.VMEM((nt // 2 * S_PITCH, LANES), I32),
            pltpu.VMEM((nt, PEER_HEADS * PEER_TOPK), I32),
            pltpu.VMEM((nt, PEER_HEADS * PEER_TOPK), F32),
            pltpu.VMEM((nt, EXP_CHUNK), BF16),
            pltpu.VMEM((nt, d), F32),
        ],
        compiler_params=pltpu.CompilerParams(
            dimension_semantics=("arbitrary", "arbitrary"), vmem_limit_bytes=VMEM_LIMIT),
        name="expert",
    )(h2, eT, gT, u_bf, v_bf, x1, g2, fnorm)


def _run_group(x, mod, pool0, conv0, h0, wts, *, bb, seg, pos0, per_token_mod):
    bsz, t, d = x.shape
    sh1, sc1, g1, sh2, sc2, g2 = jnp.split(mod, 6, axis=-1)
    n = bsz * t

    def expand(m, tile_rows):
        if per_token_mod:
            return jnp.broadcast_to(m[:, None, :], (bsz, t, d)).reshape(n // tile_rows, tile_rows, d)
        return m[:, None, :]

    pool0p = jnp.pad(pool0, ((0, 0), (POOL_HIST - pool0.shape[1], 0), (0, 0)))
    conv0p = jnp.pad(conv0, ((0, 0), (CONV_HIST - conv0.shape[1], 0), (0, 0)))
    h0e = jnp.broadcast_to(h0[:, None, :], (bsz, SUBLANES, h0.shape[-1]))
    rows = bb * seg
    x1, pool_o, conv_o, h_o = _mixer(
        x, expand(sh1, rows), expand(sc1, rows), expand(g1, rows), pool0p, conv0p, h0e, wts,
        bb=bb, seg=seg, pos0=pos0, per_token_mod=per_token_mod)
    x1f = x1.reshape(n, d)
    tok_per_mod_r = ROUTE_TOK if per_token_mod else t
    h2, eT, gT = _route(x1f, expand(sh2, ROUTE_TOK), expand(sc2, ROUTE_TOK), wts["norm2"],
                        wts["w_q"], wts["keys"], tok_per_mod=tok_per_mod_r)
    tok_per_mod_e = EXP_TOK if per_token_mod else t
    y = _expert(h2, eT, gT, wts["u"], wts["v"], x1f, expand(g2, EXP_TOK), wts["fnorm"],
                tok_per_mod=tok_per_mod_e)
    return (y.reshape(bsz, t, d), pool_o[:, 1:], conv_o[:, CONV_HIST - (CONV_WIDTH - 1):],
            h_o[:, SUBLANES - 1])


def kernel(x_prompt, x_sample, c_prompt, c_sample, state_pool, state_conv, state_lru, norm1, norm2, w_ada, b_ada, w_in, b_gate, pool_mix, pool_scale, conv_w, conv_b, w_rg, b_rg, w_ig, b_ig, lru_lambda, w_branch_a, w_branch_b, w_out, w_q, sub_keys, expert_u, expert_v, final_norm):
    depth = w_in.shape[0]
    assert depth == 1, "single-layer trunk"
    bp, tp, d = x_prompt.shape
    bs, ts, _ = x_sample.shape
    l = 0
    mod = _ada(jnp.concatenate([c_prompt, c_sample], axis=0), w_ada[l], b_ada[l])
    row = lambda v: v.reshape(1, -1)
    wts = {
        "norm1": row(norm1[l]), "norm2": row(norm2[l]), "fnorm": row(final_norm),
        "w_in": w_in[l].astype(BF16), "b_gate": row(b_gate[l]),
        "pool_mix": pool_mix[l].astype(BF16), "pool_scale": row(pool_scale[l]),
        "conv_w": conv_w[l], "conv_b": row(conv_b[l]),
        "w_gi": jnp.concatenate([w_rg[l], w_ig[l]], axis=-1).astype(BF16),
        "b_rg": row(b_rg[l]), "b_ig": row(b_ig[l]), "lam": row(lru_lambda[l]),
        "w_a": w_branch_a[l].astype(BF16), "w_b": w_branch_b[l].astype(BF16),
        "w_out": w_out[l].astype(BF16), "w_q": w_q[l].astype(BF16),
        "keys": sub_keys[l].astype(BF16),
        "u": expert_u[l].astype(BF16), "v": expert_v[l].astype(BF16),
    }
    zp = jnp.zeros((bp, state_pool.shape[2], state_pool.shape[3]), F32)
    zc = jnp.zeros((bp, state_conv.shape[2], state_conv.shape[3]), F32)
    zh = jnp.zeros((bp, state_lru.shape[2]), F32)
    y_p, pool_p, conv_p, lru_p = _run_group(
        x_prompt, mod[:bp], zp, zc, zh, wts, bb=1, seg=256, pos0=0, per_token_mod=False)
    y_s, pool_s, conv_s, lru_s = _run_group(
        x_sample, mod[bp:], state_pool[l], state_conv[l], state_lru[l], wts,
        bb=32, seg=ts, pos0=PAST_LEN, per_token_mod=True)
    st = lambda v: v[None]
    return (y_p, y_s, st(pool_p), st(conv_p), st(lru_p).astype(state_lru.dtype),
            st(pool_s), st(conv_s), st(lru_s).astype(state_lru.dtype))
```

```python
import functools

import jax
import jax.numpy as jnp
from jax import lax
from jax.experimental import pallas as pl
from jax.experimental.pallas import tpu as pltpu

F32 = jnp.float32
BF16 = jnp.bfloat16
I32 = jnp.int32

NORM_EPS = 1e-6
PAST_LEN = 16384
LRU_C = 8.0
POOL_WINDOWS = (2, 4, 8, 16)
POOL_HIST = 16
CONV_WIDTH = 4
CONV_HIST = 8
LRU_HEADS = 8
PEER_HEADS = 8
PEER_NKEYS = 128
PEER_TOPK = 16
SUBLANES = 8
LANES = 128
VMEM_LIMIT = 56 * 1024 * 1024

EXP_TOK = 512
EXP_CHUNK = 2048
EXP_SUB = 256
PAIR_UNROLL = 32
S_PITCH = 136
ROUTE_TOK = 1024


def _dot(a, b):
    return jnp.dot(a, b, preferred_element_type=F32)


def _dot_nt(a, b):
    return lax.dot_general(a, b, (((1,), (1,)), ((), ())), preferred_element_type=F32)


def _rmsnorm(x, g):
    return x * lax.rsqrt(jnp.mean(x * x, axis=-1, keepdims=True) + NORM_EPS) * g


def _gelu(x):
    return jax.nn.gelu(x, approximate=True)


def _ada_kernel(c_ref, w_ref, b_ref, o_ref):
    c = c_ref[...]
    s = (c * jax.nn.sigmoid(c)).astype(BF16)
    o_ref[...] = _dot(s, w_ref[...].astype(BF16)) + b_ref[...]


def _ada(c, w_ada, b_ada):
    n, d = c.shape
    dout = w_ada.shape[1]
    tn = 1024
    return pl.pallas_call(
        _ada_kernel,
        grid=(dout // tn,),
        in_specs=[
            pl.BlockSpec((n, d), lambda j: (0, 0)),
            pl.BlockSpec((d, tn), lambda j: (0, j)),
            pl.BlockSpec((1, tn), lambda j: (0, j)),
        ],
        out_specs=pl.BlockSpec((n, tn), lambda j: (0, j)),
        out_shape=jax.ShapeDtypeStruct((n, dout), F32),
        compiler_params=pltpu.CompilerParams(
            dimension_semantics=("arbitrary",), vmem_limit_bytes=VMEM_LIMIT),
        name="ada",
    )(c, w_ada, b_ada.reshape(1, dout))


def _mixer_kernel(x_ref, sh_ref, sc_ref, g1_ref, pool0_ref, conv0_ref, h0_ref,
                  norm1_ref, w_in_ref, b_gate_ref, pool_mix_ref, pool_scale_ref,
                  conv_w_ref, conv_b_ref, w_gi_ref, b_rg_ref, b_ig_ref, lam_ref,
                  w_a_ref, w_b_ref, w_out_ref,
                  x1_ref, pool_out_ref, conv_out_ref, h_out_ref,
                  ext_u, ext_c, gate_s, a_s, b_s, hs_s, hcar, proj_s,
                  *, bb, seg, pos0, width):
    rows = bb * seg
    tt = pl.program_id(1)
    w = width
    gdim = w // len(POOL_WINDOWS)
    hdim = w // LRU_HEADS

    @pl.when(tt == 0)
    def _():
        ext_u[:, 0:POOL_HIST, :] = pool0_ref[...]
        ext_c[:, 0:CONV_HIST, :] = conv0_ref[...]
        hcar[...] = h0_ref[...].reshape(bb * SUBLANES, w)

    x = x_ref[...].reshape(rows, w)
    h = _rmsnorm(x, norm1_ref[...]) * (1.0 + sc_ref[0]) + sh_ref[0]
    hb = h.astype(BF16)

    row = lax.broadcasted_iota(I32, (rows, 1), 0)
    pos = pos0 + tt * seg + (row & (seg - 1))

    proj_s[...] = _dot(hb, w_in_ref[...])
    u = proj_s[:, 0:w]
    ext_u[:, POOL_HIST:POOL_HIST + seg, :] = u.reshape(bb, seg, w)
    ext_rows = bb * (POOL_HIST + seg)
    wsum = ext_u[:, :, 0:w].reshape(ext_rows, w)
    level = 1
    for g, win in enumerate(POOL_WINDOWS):
        lo = g * gdim
        while level < win:
            wsum = wsum + pltpu.roll(wsum, level, 0)
            level *= 2
        acc = wsum[:, 0:gdim].reshape(bb, POOL_HIST + seg, gdim)[:, POOL_HIST:, :].reshape(rows, gdim)
        wsum = wsum[:, gdim:]
        cnt = jnp.minimum(pos + 1, win).astype(F32)
        pooled = acc / cnt - u[:, lo:lo + gdim]
        zg = _dot(pooled.astype(BF16), pool_mix_ref[g])
        gate_s[:, lo:lo + gdim] = zg
    z = gate_s[...] * pool_scale_ref[...]
    y_a = _dot(z.astype(BF16), w_a_ref[...])
    pool_out_ref[...] = ext_u[:, seg:seg + POOL_HIST, :]
    ext_u[:, 0:POOL_HIST, :] = ext_u[:, seg:seg + POOL_HIST, :]

    g_a = jax.nn.sigmoid(proj_s[:, 3 * w:4 * w] + b_gate_ref[:, 0:w])
    m = g_a * y_a

    xb = proj_s[:, w:2 * w]
    ext_c[:, CONV_HIST:CONV_HIST + seg, :] = xb.reshape(bb, seg, w)
    xc = conv_b_ref[...] + xb * conv_w_ref[CONV_WIDTH - 1:CONV_WIDTH, :]
    for k in range(CONV_WIDTH - 1):
        sh = CONV_WIDTH - 1 - k
        xc = xc + ext_c[:, CONV_HIST - sh:CONV_HIST - sh + seg, :].reshape(rows, w) * conv_w_ref[k:k + 1, :]
    conv_out_ref[...] = ext_c[:, seg:seg + CONV_HIST, :]
    ext_c[:, 0:CONV_HIST, :] = ext_c[:, seg:seg + CONV_HIST, :]

    xcb = xc.astype(BF16)
    lam = -lam_ref[...]
    softplus = jnp.maximum(lam, 0.0) + jnp.log1p(jnp.exp(-jnp.abs(lam)))
    for hh in range(LRU_HEADS):
        lo = hh * hdim
        ri = _dot(xcb[:, lo:lo + hdim], w_gi_ref[hh])
        r = jax.nn.sigmoid(ri[:, 0:hdim] + b_rg_ref[:, lo:lo + hdim])
        i = jax.nn.sigmoid(ri[:, hdim:2 * hdim] + b_ig_ref[:, lo:lo + hdim])
        log_a = (-LRU_C) * r * softplus[:, lo:lo + hdim]
        a = jnp.exp(log_a)
        mult = jnp.sqrt(-jnp.tanh(log_a) * (a * a + 1.0))
        mult = jnp.where(pos == 0, 1.0, mult)
        a_s[:, lo:lo + hdim] = a
        b_s[:, lo:lo + hdim] = mult * i * xc[:, lo:lo + hdim]

    tiles = rows // SUBLANES
    a = a_s[...].reshape(tiles, SUBLANES, w)
    b = b_s[...].reshape(tiles, SUBLANES, w)
    rin = lax.broadcasted_iota(I32, (1, SUBLANES, 1), 1)
    for d in (1, 2, 4):
        a_sh = jnp.where(rin >= d, pltpu.roll(a, d, 1), 1.0)
        b_sh = jnp.where(rin >= d, pltpu.roll(b, d, 1), 0.0)
        b = a * b_sh + b
        a = a * a_sh
    a = a.reshape(rows, w)
    b = b.reshape(rows, w)
    if seg == SUBLANES:
        hs = a * hcar[...] + b
        h_out_ref[...] = hs.reshape(bb, SUBLANES, w)
    else:
        a_s[...] = a
        b_s[...] = b

        def tile_step(k, hp):
            r0 = pl.multiple_of(k * SUBLANES, SUBLANES)
            ht = a_s[pl.ds(r0, SUBLANES), :] * hp + b_s[pl.ds(r0, SUBLANES), :]
            hs_s[pl.ds(r0, SUBLANES), :] = ht
            return jnp.broadcast_to(ht[SUBLANES - 1:SUBLANES, :], (SUBLANES, w))

        hp = lax.fori_loop(0, rows // SUBLANES, tile_step, hcar[...])
        hcar[...] = hp
        h_out_ref[...] = hp.reshape(bb, SUBLANES, w)
        hs = hs_s[...]

    yb = proj_s[:, 2 * w:3 * w]
    y_b = _dot((hs * _gelu(yb)).astype(BF16), w_b_ref[...])
    g_b = jax.nn.sigmoid(proj_s[:, 4 * w:5 * w] + b_gate_ref[:, w:2 * w])
    m = m + g_b * y_b
    out = _dot(m.astype(BF16), w_out_ref[...])
    x1_ref[...] = (x + g1_ref[0] * out).reshape(bb, seg, w)


def _const_spec(shape):
    nd = len(shape)
    return pl.BlockSpec(shape, lambda *_: (0,) * nd, pipeline_mode=pl.Buffered(1))


def _mixer(x, sh, sc, g1, pool0, conv0, h0e, weights, *, bb, seg, pos0, per_token_mod):
    bsz, t, d = x.shape
    w = d
    rows = bb * seg
    nb, nt = bsz // bb, t // seg
    rm = sh.shape[1]

    def mod_map(b, tt):
        return ((b * nt + tt) if per_token_mod else b, 0, 0)

    mod_spec = pl.BlockSpec((1, rm, d), mod_map)
    wnames = ("norm1", "w_in", "b_gate", "pool_mix", "pool_scale", "conv_w", "conv_b",
              "w_gi", "b_rg", "b_ig", "lam", "w_a", "w_b", "w_out")
    wvals = [weights[k] for k in wnames]
    kern = functools.partial(_mixer_kernel, bb=bb, seg=seg, pos0=pos0, width=w)
    return pl.pallas_call(
        kern,
        grid=(nb, nt),
        in_specs=[
            pl.BlockSpec((bb, seg, d), lambda b, tt: (b, tt, 0)),
            mod_spec, mod_spec, mod_spec,
            pl.BlockSpec((bb, POOL_HIST, w), lambda b, tt: (b, 0, 0)),
            pl.BlockSpec((bb, CONV_HIST, w), lambda b, tt: (b, 0, 0)),
            pl.BlockSpec((bb, SUBLANES, w), lambda b, tt: (b, 0, 0)),
        ] + [_const_spec(v.shape) for v in wvals],
        out_specs=[
            pl.BlockSpec((bb, seg, d), lambda b, tt: (b, tt, 0)),
            pl.BlockSpec((bb, POOL_HIST, w), lambda b, tt: (b, 0, 0)),
            pl.BlockSpec((bb, CONV_HIST, w), lambda b, tt: (b, 0, 0)),
            pl.BlockSpec((bb, SUBLANES, w), lambda b, tt: (b, 0, 0)),
        ],
        out_shape=[
            jax.ShapeDtypeStruct((bsz, t, d), F32),
            jax.ShapeDtypeStruct((bsz, POOL_HIST, w), F32),
            jax.ShapeDtypeStruct((bsz, CONV_HIST, w), F32),
            jax.ShapeDtypeStruct((bsz, SUBLANES, w), F32),
        ],
        scratch_shapes=[
            pltpu.VMEM((bb, POOL_HIST + seg, w), F32),
            pltpu.VMEM((bb, CONV_HIST + seg, w), F32),
            pltpu.VMEM((rows, w), F32),
            pltpu.VMEM((rows, w), F32),
            pltpu.VMEM((rows, w), F32),
            pltpu.VMEM((rows, w), F32),
            pltpu.VMEM((bb * SUBLANES, w), F32),
            pltpu.VMEM((rows, weights["w_in"].shape[1]), F32),
        ],
        compiler_params=pltpu.CompilerParams(
            dimension_semantics=("arbitrary", "arbitrary"), vmem_limit_bytes=VMEM_LIMIT),
        name="mixer",
    )(x, sh, sc, g1, pool0, conv0, h0e, *wvals)


def _sort16_network():
    n, pairs, p = 16, [], 1
    while p < n:
        k = p
        while k >= 1:
            for j in range(k % p, n - k, 2 * k):
                for i in range(min(k, n - j - k)):
                    if (i + j) // (2 * p) == (i + j + k) // (2 * p):
                        pairs.append((i + j, i + j + k))
            k //= 2
        p *= 2
    return pairs


def _top16_rows(s, val_ref, idx_ref):
    nt = s.shape[0] // SUBLANES
    assert nt == PEER_TOPK
    sub = lax.broadcasted_iota(I32, (SUBLANES, LANES), 0).astype(F32)
    v = [s[SUBLANES * k:SUBLANES * (k + 1), :] for k in range(nt)]
    ix = [sub + float(SUBLANES * k) for k in range(nt)]
    for i, j in _sort16_network():
        swap = (v[j] > v[i]) | ((v[j] == v[i]) & (ix[j] < ix[i]))
        v[i], v[j] = jnp.where(swap, v[j], v[i]), jnp.where(swap, v[i], v[j])
        ix[i], ix[j] = jnp.where(swap, ix[j], ix[i]), jnp.where(swap, ix[i], ix[j])
    for it in range(PEER_TOPK):
        mx = jnp.max(v[0], axis=0, keepdims=True)
        best = jnp.min(jnp.where(v[0] == mx, ix[0], float(s.shape[0])), axis=0, keepdims=True)
        val_ref[it:it + 1, :] = mx
        idx_ref[it:it + 1, :] = best
        win = ix[0] == best
        for r in range(PEER_TOPK - 1 - it):
            v[r] = jnp.where(win, v[r + 1], v[r])
            ix[r] = jnp.where(win, ix[r + 1], ix[r])


def _route_kernel(x_ref, sh_ref, sc_ref, norm2_ref, wq_ref, keys_ref,
                  h2_ref, e_ref, g_ref, h2_s, q_s, *list_refs, ntok):
    hd = pl.program_id(1)
    lists = [list_refs[6 * j:6 * j + 6] for j in range(ntok // LANES)]

    @pl.when(hd == 0)
    def _():
        x = x_ref[...]
        h2 = _rmsnorm(x, norm2_ref[...]) * (1.0 + sc_ref[0]) + sh_ref[0]
        h2b = h2.astype(BF16)
        h2_s[...] = h2b
        h2_ref[...] = h2b

    q_s[...] = _dot(h2_s[...], wq_ref[...])
    k1 = keys_ref[0, 0]
    k2 = keys_ref[0, 1]
    half = PEER_NKEYS

    sub = lax.broadcasted_iota(I32, (SUBLANES, LANES), 0)
    is_row = sub < 4
    nkf = float(PEER_NKEYS)
    nexp = nkf * nkf
    depth = jnp.full((SUBLANES, LANES), 0, I32)
    for r, dep in enumerate((16, 8, 5, 4, 12, 4, 1)):
        depth = jnp.where(sub == r, dep, depth)
    subf = sub.astype(F32)
    flat0 = jnp.where(is_row, subf * PEER_TOPK, 4.0 * PEER_TOPK + (subf - 4.0)) * nexp
    dflat = jnp.where(is_row, 1.0, float(PEER_TOPK)) * nexp

    def sub_block(j):
        v1_s, i1_s, v2_s, i2_s, bs_s, bk_s = lists[j]
        qj = q_s[j * LANES:(j + 1) * LANES, :].astype(BF16)
        _top16_rows(_dot_nt(k1, qj[:, 0:half]), v1_s, i1_s)
        _top16_rows(_dot_nt(k2, qj[:, half:2 * half]), v2_s, i2_s)
        vfix = jnp.where(is_row, v1_s[0:SUBLANES, :], pltpu.roll(v2_s[0:SUBLANES, :], 4, 0))
        efix = jnp.where(is_row, i1_s[0:SUBLANES, :] * nkf, pltpu.roll(i2_s[0:SUBLANES, :], 4, 0))
        t, k = [], []
        for d in range(PEER_TOPK):
            if d + 4 < PEER_TOPK:
                vrun = jnp.where(is_row, v2_s[d:d + 1, :], v1_s[d + 4:d + 5, :])
                erun = jnp.where(is_row, i2_s[d:d + 1, :], i1_s[d + 4:d + 5, :] * nkf)
            else:
                vrun = v2_s[d:d + 1, :]
                erun = i2_s[d:d + 1, :]
            t.append(jnp.where(depth > d, vfix + vrun, -jnp.inf))
            k.append(flat0 + dflat * float(d) + (efix + erun))
        for it in range(PEER_TOPK):
            mx = jnp.max(t[0], axis=0, keepdims=True)
            kx = jnp.min(jnp.where(t[0] == mx, k[0], PEER_TOPK * PEER_TOPK * nexp), axis=0, keepdims=True)
            bk_s[it:it + 1, :] = kx
            bs_s[it:it + 1, :] = mx
            win = k[0] == kx
            for r in range(PEER_TOPK - 1 - it):
                t[r] = jnp.where(win, t[r + 1], t[r])
                k[r] = jnp.where(win, k[r + 1], k[r])
        best = bs_s[...]
        ex = jnp.exp(best - best[0:1, :])
        gsm = ex / jnp.sum(ex, axis=0, keepdims=True)
        bk = bk_s[...]
        e_ref[j] = bk - jnp.floor(bk / nexp) * nexp
        g_ref[j] = gsm

    for j in range(ntok // LANES):
        sub_block(j)


def _route(x1, sh2, sc2, norm2, w_q, keys, *, tok_per_mod):
    n, d = x1.shape
    nt = ROUTE_TOK
    tiles = n // nt
    rm = sh2.shape[1]
    tiles_per_mod = tok_per_mod // nt
    mod_spec = pl.BlockSpec((1, rm, d), lambda i, hd: (i // tiles_per_mod, 0, 0))
    dk2 = w_q.shape[1] // PEER_HEADS
    kern = functools.partial(_route_kernel, ntok=nt)
    return pl.pallas_call(
        kern,
        grid=(tiles, PEER_HEADS),
        in_specs=[
            pl.BlockSpec((nt, d), lambda i, hd: (i, 0)),
            mod_spec, mod_spec,
            pl.BlockSpec((1, d), lambda i, hd: (0, 0)),
            pl.BlockSpec((d, dk2), lambda i, hd: (0, hd)),
            pl.BlockSpec((1, 2, PEER_NKEYS, dk2 // 2), lambda i, hd: (hd, 0, 0, 0)),
        ],
        out_specs=[
            pl.BlockSpec((nt, d), lambda i, hd: (i, 0)),
            pl.BlockSpec((nt // LANES, PEER_TOPK, LANES), lambda i, hd: (i, hd, 0)),
            pl.BlockSpec((nt // LANES, PEER_TOPK, LANES), lambda i, hd: (i, hd, 0)),
        ],
        out_shape=[
            jax.ShapeDtypeStruct((n, d), BF16),
            jax.ShapeDtypeStruct((n // LANES, PEER_HEADS * PEER_TOPK, LANES), F32),
            jax.ShapeDtypeStruct((n // LANES, PEER_HEADS * PEER_TOPK, LANES), F32),
        ],
        scratch_shapes=[
            pltpu.VMEM((nt, d), BF16),
            pltpu.VMEM((nt, dk2), F32),
        ] + [pltpu.VMEM((PEER_TOPK, LANES), F32)] * (6 * (nt // LANES)),
        compiler_params=pltpu.CompilerParams(
            dimension_semantics=("arbitrary", "arbitrary"), vmem_limit_bytes=VMEM_LIMIT),
        name="route",
    )(x1, sh2, sc2, norm2, w_q, keys)


def _expert_kernel(h2_ref, e_ref, g_ref, u_ref, v_ref, x1_ref, g2_ref, fnorm_ref,
                   y_ref, s_s, erow_s, grow_s, w2_s, acc_s, *, ntok):
    c = pl.program_id(1)
    nchunks = pl.num_programs(1)
    nk = PEER_NKEYS

    @pl.when(c == 0)
    def _():
        acc_s[...] = jnp.zeros_like(acc_s)
        for sb in range(ntok // LANES):
            erow_s[sb * LANES:(sb + 1) * LANES, :] = e_ref[sb].T.astype(I32)
            grow_s[sb * LANES:(sb + 1) * LANES, :] = g_ref[sb].T
        riota = lax.broadcasted_iota(I32, (nk, nk), 0)
        riota2 = lax.broadcasted_iota(I32, (2 * nk, nk), 0)

        def pair(p):
            n0 = pl.multiple_of(2 * p, 2)
            lts, rts = [], []
            for t in range(2):
                e = erow_s[pl.ds(n0 + t, 1), :]
                gv = grow_s[pl.ds(n0 + t, 1), :]
                i1 = e >> 7
                i2 = e & (nk - 1)
                lts.append(jnp.where(riota2 == 2 * i1 + t, gv, 0.0).astype(BF16))
                rts.append(jnp.where(riota == i2, 1.0, 0.0).astype(BF16))
            x = _dot_nt(jnp.concatenate(lts, axis=1), jnp.concatenate(rts, axis=1))
            words = pltpu.bitcast(x.astype(BF16), I32)
            s_s[pl.ds(pl.multiple_of(p * S_PITCH, SUBLANES), nk), :] = words

        def pair_group(pg, carry):
            for q in range(PAIR_UNROLL):
                pair(pg * PAIR_UNROLL + q)
            return carry

        lax.fori_loop(0, ntok // (2 * PAIR_UNROLL), pair_group, 0)

    h2 = h2_ref[...]
    i1_per_sub = EXP_SUB // nk
    for j in range(EXP_CHUNK // EXP_SUB):
        a = _dot_nt(h2, u_ref[j * EXP_SUB:(j + 1) * EXP_SUB, :])
        i1_0 = c * (EXP_CHUNK // nk) + j * i1_per_sub
        gs = [pltpu.bitcast(s_s[pl.ds(i1_0 + t, ntok // 2, stride=S_PITCH), :], BF16)
              for t in range(i1_per_sub)]
        g = jnp.concatenate(gs, axis=1).astype(F32)
        w2_s[:, j * EXP_SUB:(j + 1) * EXP_SUB] = (_gelu(a) * g).astype(BF16)
    acc_s[...] += _dot(w2_s[...], v_ref[...])

    @pl.when(c == nchunks - 1)
    def _():
        x2 = x1_ref[...] + g2_ref[0] * acc_s[...]
        y_ref[...] = _rmsnorm(x2, fnorm_ref[...])


def _expert(h2, eT, gT, u_bf, v_bf, x1, g2, fnorm, *, tok_per_mod):
    n, d = x1.shape
    nt = EXP_TOK
    ne = u_bf.shape[0]
    rm = g2.shape[1]
    blocks_per_mod = tok_per_mod // nt
    kern = functools.partial(_expert_kernel, ntok=nt)
    return pl.pallas_call(
        kern,
        grid=(n // nt, ne // EXP_CHUNK),
        in_specs=[
            pl.BlockSpec((nt, d), lambda i, c: (i, 0)),
            pl.BlockSpec((nt // LANES, PEER_HEADS * PEER_TOPK, LANES), lambda i, c: (i, 0, 0)),
            pl.BlockSpec((nt // LANES, PEER_HEADS * PEER_TOPK, LANES), lambda i, c: (i, 0, 0)),
            pl.BlockSpec((EXP_CHUNK, d), lambda i, c: (c, 0)),
            pl.BlockSpec((EXP_CHUNK, d), lambda i, c: (c, 0)),
            pl.BlockSpec((nt, d), lambda i, c: (i, 0)),
            pl.BlockSpec((1, rm, d), lambda i, c: (i // blocks_per_mod, 0, 0)),
            pl.BlockSpec((1, d), lambda i, c: (0, 0)),
        ],
        out_specs=pl.BlockSpec((nt, d), lambda i, c: (i, 0)),
        out_shape=jax.ShapeDtypeStruct((n, d), F32),
        scratch_shapes=[
            pltpu.VMEM((nt // 2 * S_PITCH, LANES), I32),
            pltpu.VMEM((nt, PEER_HEADS * PEER_TOPK), I32),
            pltpu.VMEM((nt, PEER_HEADS * PEER_TOPK), F32),
            pltpu.VMEM((nt, EXP_CHUNK), BF16),
            pltpu.VMEM((nt, d), F32),
        ],
        compiler_params=pltpu.CompilerParams(
            dimension_semantics=("arbitrary", "arbitrary"), vmem_limit_bytes=VMEM_LIMIT),
        name="expert",
    )(h2, eT, gT, u_bf, v_bf, x1, g2, fnorm)


def _run_group(x, mod, pool0, conv0, h0, wts, *, bb, seg, pos0, per_token_mod):
    bsz, t, d = x.shape
    sh1, sc1, g1, sh2, sc2, g2 = jnp.split(mod, 6, axis=-1)
    n = bsz * t

    def expand(m, tile_rows):
        if per_token_mod:
            return jnp.broadcast_to(m[:, None, :], (bsz, t, d)).reshape(n // tile_rows, tile_rows, d)
        return m[:, None, :]

    pool0p = jnp.pad(pool0, ((0, 0), (POOL_HIST - pool0.shape[1], 0), (0, 0)))
    conv0p = jnp.pad(conv0, ((0, 0), (CONV_HIST - conv0.shape[1], 0), (0, 0)))
    h0e = jnp.broadcast_to(h0[:, None, :], (bsz, SUBLANES, h0.shape[-1]))
    rows = bb * seg
    x1, pool_o, conv_o, h_o = _mixer(
        x, expand(sh1, rows), expand(sc1, rows), expand(g1, rows), pool0p, conv0p, h0e, wts,
        bb=bb, seg=seg, pos0=pos0, per_token_mod=per_token_mod)
    x1f = x1.reshape(n, d)
    tok_per_mod_r = ROUTE_TOK if per_token_mod else t
    h2, eT, gT = _route(x1f, expand(sh2, ROUTE_TOK), expand(sc2, ROUTE_TOK), wts["norm2"],
                        wts["w_q"], wts["keys"], tok_per_mod=tok_per_mod_r)
    tok_per_mod_e = EXP_TOK if per_token_mod else t
    y = _expert(h2, eT, gT, wts["u"], wts["v"], x1f, expand(g2, EXP_TOK), wts["fnorm"],
                tok_per_mod=tok_per_mod_e)
    return (y.reshape(bsz, t, d), pool_o[:, 1:], conv_o[:, CONV_HIST - (CONV_WIDTH - 1):],
            h_o[:, SUBLANES - 1])


def kernel(x_prompt, x_sample, c_prompt, c_sample, state_pool, state_conv, state_lru, norm1, norm2, w_ada, b_ada, w_in, b_gate, pool_mix, pool_scale, conv_w, conv_b, w_rg, b_rg, w_ig, b_ig, lru_lambda, w_branch_a, w_branch_b, w_out, w_q, sub_keys, expert_u, expert_v, final_norm):
    depth = w_in.shape[0]
    assert depth == 1, "single-layer trunk"
    bp, tp, d = x_prompt.shape
    bs, ts, _ = x_sample.shape
    l = 0
    mod = _ada(jnp.concatenate([c_prompt, c_sample], axis=0), w_ada[l], b_ada[l])
    row = lambda v: v.reshape(1, -1)
    wts = {
        "norm1": row(norm1[l]), "norm2": row(norm2[l]), "fnorm": row(final_norm),
        "w_in": w_in[l].astype(BF16), "b_gate": row(b_gate[l]),
        "pool_mix": pool_mix[l].astype(BF16), "pool_scale": row(pool_scale[l]),
        "conv_w": conv_w[l], "conv_b": row(conv_b[l]),
        "w_gi": jnp.concatenate([w_rg[l], w_ig[l]], axis=-1).astype(BF16),
        "b_rg": row(b_rg[l]), "b_ig": row(b_ig[l]), "lam": row(lru_lambda[l]),
        "w_a": w_branch_a[l].astype(BF16), "w_b": w_branch_b[l].astype(BF16),
        "w_out": w_out[l].astype(BF16), "w_q": w_q[l].astype(BF16),
        "keys": sub_keys[l].astype(BF16),
        "u": expert_u[l].astype(BF16), "v": expert_v[l].astype(BF16),
    }
    zp = jnp.zeros((bp, state_pool.shape[2], state_pool.shape[3]), F32)
    zc = jnp.zeros((bp, state_conv.shape[2], state_conv.shape[3]), F32)
    zh = jnp.zeros((bp, state_lru.shape[2]), F32)
    y_p, pool_p, conv_p, lru_p = _run_group(
        x_prompt, mod[:bp], zp, zc, zh, wts, bb=1, seg=256, pos0=0, per_token_mod=False)
    y_s, pool_s, conv_s, lru_s = _run_group(
        x_sample, mod[bp:], state_pool[l], state_conv[l], state_lru[l], wts,
        bb=32, seg=ts, pos0=PAST_LEN, per_token_mod=True)
    st = lambda v: v[None]
    return (y_p, y_s, st(pool_p), st(conv_p), st(lru_p).astype(state_lru.dtype),
            st(pool_s), st(conv_s), st(lru_s).astype(state_lru.dtype))
```

```python
import functools

import jax
import jax.numpy as jnp
from jax import lax
from jax.experimental import pallas as pl
from jax.experimental.pallas import tpu as pltpu

F32 = jnp.float32
BF16 = jnp.bfloat16
I32 = jnp.int32

NORM_EPS = 1e-6
PAST_LEN = 16384
LRU_C = 8.0
POOL_WINDOWS = (2, 4, 8, 16)
POOL_HIST = 16
CONV_WIDTH = 4
CONV_HIST = 8
LRU_HEADS = 8
PEER_HEADS = 8
PEER_NKEYS = 128
PEER_TOPK = 16
SUBLANES = 8
LANES = 128
VMEM_LIMIT = 56 * 1024 * 1024

EXP_TOK = 512
EXP_CHUNK = 2048
EXP_SUB = 256
PAIR_UNROLL = 64
S_PITCH = 136
ROUTE_TOK = 1024
ROUTE_HEADS = 2


def _dot(a, b):
    return jnp.dot(a, b, preferred_element_type=F32)


def _dot_nt(a, b):
    return lax.dot_general(a, b, (((1,), (1,)), ((), ())), preferred_element_type=F32)


def _rmsnorm(x, g):
    return x * lax.rsqrt(jnp.mean(x * x, axis=-1, keepdims=True) + NORM_EPS) * g


def _gelu(x):
    return jax.nn.gelu(x, approximate=True)


def _ada_kernel(c_ref, w_ref, b_ref, o_ref):
    c = c_ref[...]
    s = (c * jax.nn.sigmoid(c)).astype(BF16)
    o_ref[...] = _dot(s, w_ref[...].astype(BF16)) + b_ref[...]


def _ada(c, w_ada, b_ada):
    n, d = c.shape
    dout = w_ada.shape[1]
    tn = 1024
    return pl.pallas_call(
        _ada_kernel,
        grid=(dout // tn,),
        in_specs=[
            pl.BlockSpec((n, d), lambda j: (0, 0)),
            pl.BlockSpec((d, tn), lambda j: (0, j)),
            pl.BlockSpec((1, tn), lambda j: (0, j)),
        ],
        out_specs=pl.BlockSpec((n, tn), lambda j: (0, j)),
        out_shape=jax.ShapeDtypeStruct((n, dout), F32),
        compiler_params=pltpu.CompilerParams(
            dimension_semantics=("arbitrary",), vmem_limit_bytes=VMEM_LIMIT),
        name="ada",
    )(c, w_ada, b_ada.reshape(1, dout))


def _mixer_kernel(x_ref, sh_ref, sc_ref, g1_ref, pool0_ref, conv0_ref, h0_ref,
                  norm1_ref, w_in_ref, b_gate_ref, pool_mix_ref, pool_scale_ref,
                  conv_w_ref, conv_b_ref, w_gi_ref, b_rg_ref, b_ig_ref, lam_ref,
                  w_a_ref, w_b_ref, w_out_ref,
                  x1_ref, pool_out_ref, conv_out_ref, h_out_ref,
                  ext_u, ext_c, gate_s, a_s, b_s, hs_s, hcar, proj_s,
                  *, bb, seg, pos0, width):
    rows = bb * seg
    tt = pl.program_id(1)
    w = width
    gdim = w // len(POOL_WINDOWS)
    hdim = w // LRU_HEADS

    @pl.when(tt == 0)
    def _():
        ext_u[:, 0:POOL_HIST, :] = pool0_ref[...]
        ext_c[:, 0:CONV_HIST, :] = conv0_ref[...]
        hcar[...] = h0_ref[...].reshape(bb * SUBLANES, w)

    x = x_ref[...].reshape(rows, w)
    h = _rmsnorm(x, norm1_ref[...]) * (1.0 + sc_ref[0]) + sh_ref[0]
    hb = h.astype(BF16)

    row = lax.broadcasted_iota(I32, (rows, 1), 0)
    pos = pos0 + tt * seg + (row & (seg - 1))

    proj_s[...] = _dot(hb, w_in_ref[...])
    u = proj_s[:, 0:w]
    ext_u[:, POOL_HIST:POOL_HIST + seg, :] = u.reshape(bb, seg, w)
    ext_rows = bb * (POOL_HIST + seg)
    wsum = ext_u[:, :, 0:w].reshape(ext_rows, w)
    level = 1
    for g, win in enumerate(POOL_WINDOWS):
        lo = g * gdim
        while level < win:
            wsum = wsum + pltpu.roll(wsum, level, 0)
            level *= 2
        acc = wsum[:, 0:gdim].reshape(bb, POOL_HIST + seg, gdim)[:, POOL_HIST:, :].reshape(rows, gdim)
        wsum = wsum[:, gdim:]
        cnt = jnp.minimum(pos + 1, win).astype(F32)
        pooled = acc / cnt - u[:, lo:lo + gdim]
        zg = _dot(pooled.astype(BF16), pool_mix_ref[g])
        gate_s[:, lo:lo + gdim] = zg
    z = gate_s[...] * pool_scale_ref[...]
    y_a = _dot(z.astype(BF16), w_a_ref[...])
    pool_out_ref[...] = ext_u[:, seg:seg + POOL_HIST, :]
    ext_u[:, 0:POOL_HIST, :] = ext_u[:, seg:seg + POOL_HIST, :]

    g_a = jax.nn.sigmoid(proj_s[:, 3 * w:4 * w] + b_gate_ref[:, 0:w])
    m = g_a * y_a

    xb = proj_s[:, w:2 * w]
    ext_c[:, CONV_HIST:CONV_HIST + seg, :] = xb.reshape(bb, seg, w)
    xc = conv_b_ref[...] + xb * conv_w_ref[CONV_WIDTH - 1:CONV_WIDTH, :]
    for k in range(CONV_WIDTH - 1):
        sh = CONV_WIDTH - 1 - k
        xc = xc + ext_c[:, CONV_HIST - sh:CONV_HIST - sh + seg, :].reshape(rows, w) * conv_w_ref[k:k + 1, :]
    conv_out_ref[...] = ext_c[:, seg:seg + CONV_HIST, :]
    ext_c[:, 0:CONV_HIST, :] = ext_c[:, seg:seg + CONV_HIST, :]

    xcb = xc.astype(BF16)
    lam = -lam_ref[...]
    softplus = jnp.maximum(lam, 0.0) + jnp.log1p(jnp.exp(-jnp.abs(lam)))
    for hh in range(LRU_HEADS):
        lo = hh * hdim
        ri = _dot(xcb[:, lo:lo + hdim], w_gi_ref[hh])
        r = jax.nn.sigmoid(ri[:, 0:hdim] + b_rg_ref[:, lo:lo + hdim])
        i = jax.nn.sigmoid(ri[:, hdim:2 * hdim] + b_ig_ref[:, lo:lo + hdim])
        log_a = (-LRU_C) * r * softplus[:, lo:lo + hdim]
        a = jnp.exp(log_a)
        mult = jnp.sqrt(-jnp.tanh(log_a) * (a * a + 1.0))
        mult = jnp.where(pos == 0, 1.0, mult)
        a_s[:, lo:lo + hdim] = a
        b_s[:, lo:lo + hdim] = mult * i * xc[:, lo:lo + hdim]

    tiles = rows // SUBLANES
    a = a_s[...].reshape(tiles, SUBLANES, w)
    b = b_s[...].reshape(tiles, SUBLANES, w)
    rin = lax.broadcasted_iota(I32, (1, SUBLANES, 1), 1)
    for d in (1, 2, 4):
        a_sh = jnp.where(rin >= d, pltpu.roll(a, d, 1), 1.0)
        b_sh = jnp.where(rin >= d, pltpu.roll(b, d, 1), 0.0)
        b = a * b_sh + b
        a = a * a_sh
    a = a.reshape(rows, w)
    b = b.reshape(rows, w)
    if seg == SUBLANES:
        hs = a * hcar[...] + b
        h_out_ref[...] = hs.reshape(bb, SUBLANES, w)
    else:
        a_s[...] = a
        b_s[...] = b

        def tile_step(k, hp):
            r0 = pl.multiple_of(k * SUBLANES, SUBLANES)
            ht = a_s[pl.ds(r0, SUBLANES), :] * hp + b_s[pl.ds(r0, SUBLANES), :]
            hs_s[pl.ds(r0, SUBLANES), :] = ht
            return jnp.broadcast_to(ht[SUBLANES - 1:SUBLANES, :], (SUBLANES, w))

        hp = lax.fori_loop(0, rows // SUBLANES, tile_step, hcar[...])
        hcar[...] = hp
        h_out_ref[...] = hp.reshape(bb, SUBLANES, w)
        hs = hs_s[...]

    yb = proj_s[:, 2 * w:3 * w]
    y_b = _dot((hs * _gelu(yb)).astype(BF16), w_b_ref[...])
    g_b = jax.nn.sigmoid(proj_s[:, 4 * w:5 * w] + b_gate_ref[:, w:2 * w])
    m = m + g_b * y_b
    out = _dot(m.astype(BF16), w_out_ref[...])
    x1_ref[...] = (x + g1_ref[0] * out).reshape(bb, seg, w)


def _const_spec(shape):
    nd = len(shape)
    return pl.BlockSpec(shape, lambda *_: (0,) * nd, pipeline_mode=pl.Buffered(1))


def _mixer(x, sh, sc, g1, pool0, conv0, h0e, weights, *, bb, seg, pos0, per_token_mod):
    bsz, t, d = x.shape
    w = d
    rows = bb * seg
    nb, nt = bsz // bb, t // seg
    rm = sh.shape[1]

    def mod_map(b, tt):
        return ((b * nt + tt) if per_token_mod else b, 0, 0)

    mod_spec = pl.BlockSpec((1, rm, d), mod_map)
    wnames = ("norm1", "w_in", "b_gate", "pool_mix", "pool_scale", "conv_w", "conv_b",
              "w_gi", "b_rg", "b_ig", "lam", "w_a", "w_b", "w_out")
    wvals = [weights[k] for k in wnames]
    kern = functools.partial(_mixer_kernel, bb=bb, seg=seg, pos0=pos0, width=w)
    return pl.pallas_call(
        kern,
        grid=(nb, nt),
        in_specs=[
            pl.BlockSpec((bb, seg, d), lambda b, tt: (b, tt, 0)),
            mod_spec, mod_spec, mod_spec,
            pl.BlockSpec((bb, POOL_HIST, w), lambda b, tt: (b, 0, 0)),
            pl.BlockSpec((bb, CONV_HIST, w), lambda b, tt: (b, 0, 0)),
            pl.BlockSpec((bb, SUBLANES, w), lambda b, tt: (b, 0, 0)),
        ] + [_const_spec(v.shape) for v in wvals],
        out_specs=[
            pl.BlockSpec((bb, seg, d), lambda b, tt: (b, tt, 0)),
            pl.BlockSpec((bb, POOL_HIST, w), lambda b, tt: (b, 0, 0)),
            pl.BlockSpec((bb, CONV_HIST, w), lambda b, tt: (b, 0, 0)),
            pl.BlockSpec((bb, SUBLANES, w), lambda b, tt: (b, 0, 0)),
        ],
        out_shape=[
            jax.ShapeDtypeStruct((bsz, t, d), F32),
            jax.ShapeDtypeStruct((bsz, POOL_HIST, w), F32),
            jax.ShapeDtypeStruct((bsz, CONV_HIST, w), F32),
            jax.ShapeDtypeStruct((bsz, SUBLANES, w), F32),
        ],
        scratch_shapes=[
            pltpu.VMEM((bb, POOL_HIST + seg, w), F32),
            pltpu.VMEM((bb, CONV_HIST + seg, w), F32),
            pltpu.VMEM((rows, w), F32),
            pltpu.VMEM((rows, w), F32),
            pltpu.VMEM((rows, w), F32),
            pltpu.VMEM((rows, w), F32),
            pltpu.VMEM((bb * SUBLANES, w), F32),
            pltpu.VMEM((rows, weights["w_in"].shape[1]), F32),
        ],
        compiler_params=pltpu.CompilerParams(
            dimension_semantics=("arbitrary", "arbitrary"), vmem_limit_bytes=VMEM_LIMIT),
        name="mixer",
    )(x, sh, sc, g1, pool0, conv0, h0e, *wvals)


def _sort16_network():
    n, pairs, p = 16, [], 1
    while p < n:
        k = p
        while k >= 1:
            for j in range(k % p, n - k, 2 * k):
                for i in range(min(k, n - j - k)):
                    if (i + j) // (2 * p) == (i + j + k) // (2 * p):
                        pairs.append((i + j, i + j + k))
            k //= 2
        p *= 2
    return pairs


def _top16_rows(s, val_ref, idx_ref):
    nt = s.shape[0] // SUBLANES
    assert nt == PEER_TOPK
    sub = lax.broadcasted_iota(I32, (SUBLANES, LANES), 0).astype(F32)
    v = [s[SUBLANES * k:SUBLANES * (k + 1), :] for k in range(nt)]
    ix = [sub + float(SUBLANES * k) for k in range(nt)]
    for i, j in _sort16_network():
        swap = (v[j] > v[i]) | ((v[j] == v[i]) & (ix[j] < ix[i]))
        v[i], v[j] = jnp.where(swap, v[j], v[i]), jnp.where(swap, v[i], v[j])
        ix[i], ix[j] = jnp.where(swap, ix[j], ix[i]), jnp.where(swap, ix[i], ix[j])
    for it in range(PEER_TOPK):
        mx = jnp.max(v[0], axis=0, keepdims=True)
        best = jnp.min(jnp.where(v[0] == mx, ix[0], float(s.shape[0])), axis=0, keepdims=True)
        val_ref[it:it + 1, :] = mx
        idx_ref[it:it + 1, :] = best
        win = ix[0] == best
        for r in range(PEER_TOPK - 1 - it):
            v[r] = jnp.where(win, v[r + 1], v[r])
            ix[r] = jnp.where(win, ix[r + 1], ix[r])


def _route_kernel(x_ref, sh_ref, sc_ref, norm2_ref, wq_ref, keys_ref,
                  h2_ref, e_ref, g_ref, h2_s, q_s, *list_refs, ntok):
    hd = pl.program_id(1)
    nsub = ntok // LANES
    lists = [list_refs[6 * u:6 * u + 6] for u in range(ROUTE_HEADS * nsub)]

    @pl.when(hd == 0)
    def _():
        x = x_ref[...]
        h2 = _rmsnorm(x, norm2_ref[...]) * (1.0 + sc_ref[0]) + sh_ref[0]
        h2b = h2.astype(BF16)
        h2_s[...] = h2b
        h2_ref[...] = h2b

    q_s[...] = _dot(h2_s[...], wq_ref[...])
    half = PEER_NKEYS

    sub = lax.broadcasted_iota(I32, (SUBLANES, LANES), 0)
    is_row = sub < 4
    nkf = float(PEER_NKEYS)
    nexp = nkf * nkf
    depth = jnp.full((SUBLANES, LANES), 0, I32)
    for r, dep in enumerate((16, 8, 5, 4, 12, 4, 1)):
        depth = jnp.where(sub == r, dep, depth)
    subf = sub.astype(F32)
    flat0 = jnp.where(is_row, subf * PEER_TOPK, 4.0 * PEER_TOPK + (subf - 4.0)) * nexp
    dflat = jnp.where(is_row, 1.0, float(PEER_TOPK)) * nexp

    def sub_block(hh, j):
        v1_s, i1_s, v2_s, i2_s, bs_s, bk_s = lists[hh * nsub + j]
        qj = q_s[j * LANES:(j + 1) * LANES, 2 * half * hh:2 * half * (hh + 1)].astype(BF16)
        _top16_rows(_dot_nt(keys_ref[hh, 0], qj[:, 0:half]), v1_s, i1_s)
        _top16_rows(_dot_nt(keys_ref[hh, 1], qj[:, half:2 * half]), v2_s, i2_s)
        vfix = jnp.where(is_row, v1_s[0:SUBLANES, :], pltpu.roll(v2_s[0:SUBLANES, :], 4, 0))
        efix = jnp.where(is_row, i1_s[0:SUBLANES, :] * nkf, pltpu.roll(i2_s[0:SUBLANES, :], 4, 0))
        t, k = [], []
        for d in range(PEER_TOPK):
            if d + 4 < PEER_TOPK:
                vrun = jnp.where(is_row, v2_s[d:d + 1, :], v1_s[d + 4:d + 5, :])
                erun = jnp.where(is_row, i2_s[d:d + 1, :], i1_s[d + 4:d + 5, :] * nkf)
            else:
                vrun = v2_s[d:d + 1, :]
                erun = i2_s[d:d + 1, :]
            t.append(jnp.where(depth > d, vfix + vrun, -jnp.inf))
            k.append(flat0 + dflat * float(d) + (efix + erun))
        for it in range(PEER_TOPK):
            mx = jnp.max(t[0], axis=0, keepdims=True)
            kx = jnp.min(jnp.where(t[0] == mx, k[0], PEER_TOPK * PEER_TOPK * nexp), axis=0, keepdims=True)
            bk_s[it:it + 1, :] = kx
            bs_s[it:it + 1, :] = mx
            win = k[0] == kx
            for r in range(PEER_TOPK - 1 - it):
                t[r] = jnp.where(win, t[r + 1], t[r])
                k[r] = jnp.where(win, k[r + 1], k[r])
        best = bs_s[...]
        ex = jnp.exp(best - best[0:1, :])
        gsm = ex / jnp.sum(ex, axis=0, keepdims=True)
        bk = bk_s[...]
        rows = slice(hh * PEER_TOPK, (hh + 1) * PEER_TOPK)
        e_ref[j, rows, :] = bk - jnp.floor(bk / nexp) * nexp
        g_ref[j, rows, :] = gsm

    for hh in range(ROUTE_HEADS):
        for j in range(nsub):
            sub_block(hh, j)


def _route(x1, sh2, sc2, norm2, w_q, keys, *, tok_per_mod):
    n, d = x1.shape
    nt = ROUTE_TOK
    tiles = n // nt
    rm = sh2.shape[1]
    tiles_per_mod = tok_per_mod // nt
    mod_spec = pl.BlockSpec((1, rm, d), lambda i, hd: (i // tiles_per_mod, 0, 0))
    dk2 = w_q.shape[1] // PEER_HEADS
    rh = ROUTE_HEADS
    kern = functools.partial(_route_kernel, ntok=nt)
    return pl.pallas_call(
        kern,
        grid=(tiles, PEER_HEADS // rh),
        in_specs=[
            pl.BlockSpec((nt, d), lambda i, hd: (i, 0)),
            mod_spec, mod_spec,
            pl.BlockSpec((1, d), lambda i, hd: (0, 0)),
            pl.BlockSpec((d, rh * dk2), lambda i, hd: (0, hd)),
            pl.BlockSpec((rh, 2, PEER_NKEYS, dk2 // 2), lambda i, hd: (hd, 0, 0, 0)),
        ],
        out_specs=[
            pl.BlockSpec((nt, d), lambda i, hd: (i, 0)),
            pl.BlockSpec((nt // LANES, rh * PEER_TOPK, LANES), lambda i, hd: (i, hd, 0)),
            pl.BlockSpec((nt // LANES, rh * PEER_TOPK, LANES), lambda i, hd: (i, hd, 0)),
        ],
        out_shape=[
            jax.ShapeDtypeStruct((n, d), BF16),
            jax.ShapeDtypeStruct((n // LANES, PEER_HEADS * PEER_TOPK, LANES), F32),
            jax.ShapeDtypeStruct((n // LANES, PEER_HEADS * PEER_TOPK, LANES), F32),
        ],
        scratch_shapes=[
            pltpu.VMEM((nt, d), BF16),
            pltpu.VMEM((nt, rh * dk2), F32),
        ] + [pltpu.VMEM((PEER_TOPK, LANES), F32)] * (6 * rh * (nt // LANES)),
        compiler_params=pltpu.CompilerParams(
            dimension_semantics=("arbitrary", "arbitrary"), vmem_limit_bytes=VMEM_LIMIT),
        name="route",
    )(x1, sh2, sc2, norm2, w_q, keys)


def _expert_kernel(h2_ref, e_ref, g_ref, u_ref, v_ref, x1_ref, g2_ref, fnorm_ref,
                   y_ref, s_s, erow_s, grow_s, w2_s, acc_s, *, ntok):
    c = pl.program_id(1)
    nchunks = pl.num_programs(1)
    nk = PEER_NKEYS

    @pl.when(c == 0)
    def _():
        acc_s[...] = jnp.zeros_like(acc_s)
        for sb in range(ntok // LANES):
            erow_s[sb * LANES:(sb + 1) * LANES, :] = e_ref[sb].T.astype(I32)
            grow_s[sb * LANES:(sb + 1) * LANES, :] = g_ref[sb].T
        riota = lax.broadcasted_iota(I32, (nk, nk), 0)
        riota2 = lax.broadcasted_iota(I32, (2 * nk, nk), 0)

        def pair(p):
            n0 = pl.multiple_of(2 * p, 2)
            lts, rts = [], []
            for t in range(2):
                e = erow_s[pl.ds(n0 + t, 1), :]
                gv = grow_s[pl.ds(n0 + t, 1), :]
                i1 = e >> 7
                i2 = e & (nk - 1)
                lts.append(jnp.where(riota2 == 2 * i1 + t, gv, 0.0).astype(BF16))
                rts.append(jnp.where(riota == i2, 1.0, 0.0).astype(BF16))
            x = _dot_nt(jnp.concatenate(lts, axis=1), jnp.concatenate(rts, axis=1))
            words = pltpu.bitcast(x.astype(BF16), I32)
            s_s[pl.ds(pl.multiple_of(p * S_PITCH, SUBLANES), nk), :] = words

        def pair_group(pg, carry):
            for q in range(PAIR_UNROLL):
                pair(pg * PAIR_UNROLL + q)
            return carry

        lax.fori_loop(0, ntok // (2 * PAIR_UNROLL), pair_group, 0)

    h2 = h2_ref[...]
    i1_per_sub = EXP_SUB // nk
    for j in range(EXP_CHUNK // EXP_SUB):
        a = _dot_nt(h2, u_ref[j * EXP_SUB:(j + 1) * EXP_SUB, :])
        i1_0 = c * (EXP_CHUNK // nk) + j * i1_per_sub
        gs = [pltpu.bitcast(s_s[pl.ds(i1_0 + t, ntok // 2, stride=S_PITCH), :], BF16)
              for t in range(i1_per_sub)]
        g = jnp.concatenate(gs, axis=1).astype(F32)
        w2_s[:, j * EXP_SUB:(j + 1) * EXP_SUB] = (_gelu(a) * g).astype(BF16)
    acc_s[...] += _dot(w2_s[...], v_ref[...])

    @pl.when(c == nchunks - 1)
    def _():
        x2 = x1_ref[...] + g2_ref[0] * acc_s[...]
        y_ref[...] = _rmsnorm(x2, fnorm_ref[...])


def _expert(h2, eT, gT, u_bf, v_bf, x1, g2, fnorm, *, tok_per_mod):
    n, d = x1.shape
    nt = EXP_TOK
    ne = u_bf.shape[0]
    rm = g2.shape[1]
    blocks_per_mod = tok_per_mod // nt
    kern = functools.partial(_expert_kernel, ntok=nt)
    return pl.pallas_call(
        kern,
        grid=(n // nt, ne // EXP_CHUNK),
        in_specs=[
            pl.BlockSpec((nt, d), lambda i, c: (i, 0)),
            pl.BlockSpec((nt // LANES, PEER_HEADS * PEER_TOPK, LANES), lambda i, c: (i, 0, 0)),
            pl.BlockSpec((nt // LANES, PEER_HEADS * PEER_TOPK, LANES), lambda i, c: (i, 0, 0)),
            pl.BlockSpec((EXP_CHUNK, d), lambda i, c: (c, 0)),
            pl.BlockSpec((EXP_CHUNK, d), lambda i, c: (c, 0)),
            pl.BlockSpec((nt, d), lambda i, c: (i, 0)),
            pl.BlockSpec((1, rm, d), lambda i, c: (i // blocks_per_mod, 0, 0)),
            pl.BlockSpec((1, d), lambda i, c: (0, 0)),
        ],
        out_specs=pl.BlockSpec((nt, d), lambda i, c: (i, 0)),
        out_shape=jax.ShapeDtypeStruct((n, d), F32),
        scratch_shapes=[
            pltpu.VMEM((nt // 2 * S_PITCH, LANES), I32),
            pltpu.VMEM((nt, PEER_HEADS * PEER_TOPK), I32),
            pltpu.VMEM((nt, PEER_HEADS * PEER_TOPK), F32),
            pltpu.VMEM((nt, EXP_CHUNK), BF16),
            pltpu.VMEM((nt, d), F32),
        ],
        compiler_params=pltpu.CompilerParams(
            dimension_semantics=("arbitrary", "arbitrary"), vmem_limit_bytes=VMEM_LIMIT),
        name="expert",
    )(h2, eT, gT, u_bf, v_bf, x1, g2, fnorm)


def _run_group(x, mod, pool0, conv0, h0, wts, *, bb, seg, pos0, per_token_mod):
    bsz, t, d = x.shape
    sh1, sc1, g1, sh2, sc2, g2 = jnp.split(mod, 6, axis=-1)
    n = bsz * t

    def expand(m, tile_rows):
        if per_token_mod:
            return jnp.broadcast_to(m[:, None, :], (bsz, t, d)).reshape(n // tile_rows, tile_rows, d)
        return m[:, None, :]

    pool0p = jnp.pad(pool0, ((0, 0), (POOL_HIST - pool0.shape[1], 0), (0, 0)))
    conv0p = jnp.pad(conv0, ((0, 0), (CONV_HIST - conv0.shape[1], 0), (0, 0)))
    h0e = jnp.broadcast_to(h0[:, None, :], (bsz, SUBLANES, h0.shape[-1]))
    rows = bb * seg
    x1, pool_o, conv_o, h_o = _mixer(
        x, expand(sh1, rows), expand(sc1, rows), expand(g1, rows), pool0p, conv0p, h0e, wts,
        bb=bb, seg=seg, pos0=pos0, per_token_mod=per_token_mod)
    x1f = x1.reshape(n, d)
    tok_per_mod_r = ROUTE_TOK if per_token_mod else t
    h2, eT, gT = _route(x1f, expand(sh2, ROUTE_TOK), expand(sc2, ROUTE_TOK), wts["norm2"],
                        wts["w_q"], wts["keys"], tok_per_mod=tok_per_mod_r)
    tok_per_mod_e = EXP_TOK if per_token_mod else t
    y = _expert(h2, eT, gT, wts["u"], wts["v"], x1f, expand(g2, EXP_TOK), wts["fnorm"],
                tok_per_mod=tok_per_mod_e)
    return (y.reshape(bsz, t, d), pool_o[:, 1:], conv_o[:, CONV_HIST - (CONV_WIDTH - 1):],
            h_o[:, SUBLANES - 1])


def kernel(x_prompt, x_sample, c_prompt, c_sample, state_pool, state_conv, state_lru, norm1, norm2, w_ada, b_ada, w_in, b_gate, pool_mix, pool_scale, conv_w, conv_b, w_rg, b_rg, w_ig, b_ig, lru_lambda, w_branch_a, w_branch_b, w_out, w_q, sub_keys, expert_u, expert_v, final_norm):
    depth = w_in.shape[0]
    assert depth == 1, "single-layer trunk"
    bp, tp, d = x_prompt.shape
    bs, ts, _ = x_sample.shape
    l = 0
    mod = _ada(jnp.concatenate([c_prompt, c_sample], axis=0), w_ada[l], b_ada[l])
    row = lambda v: v.reshape(1, -1)
    wts = {
        "norm1": row(norm1[l]), "norm2": row(norm2[l]), "fnorm": row(final_norm),
        "w_in": w_in[l].astype(BF16), "b_gate": row(b_gate[l]),
        "pool_mix": pool_mix[l].astype(BF16), "pool_scale": row(pool_scale[l]),
        "conv_w": conv_w[l], "conv_b": row(conv_b[l]),
        "w_gi": jnp.concatenate([w_rg[l], w_ig[l]], axis=-1).astype(BF16),
        "b_rg": row(b_rg[l]), "b_ig": row(b_ig[l]), "lam": row(lru_lambda[l]),
        "w_a": w_branch_a[l].astype(BF16), "w_b": w_branch_b[l].astype(BF16),
        "w_out": w_out[l].astype(BF16), "w_q": w_q[l].astype(BF16),
        "keys": sub_keys[l].astype(BF16),
        "u": expert_u[l].astype(BF16), "v": expert_v[l].astype(BF16),
    }
    zp = jnp.zeros((bp, state_pool.shape[2], state_pool.shape[3]), F32)
    zc = jnp.zeros((bp, state_conv.shape[2], state_conv.shape[3]), F32)
    zh = jnp.zeros((bp, state_lru.shape[2]), F32)
    y_p, pool_p, conv_p, lru_p = _run_group(
        x_prompt, mod[:bp], zp, zc, zh, wts, bb=1, seg=256, pos0=0, per_token_mod=False)
    y_s, pool_s, conv_s, lru_s = _run_group(
        x_sample, mod[bp:], state_pool[l], state_conv[l], state_lru[l], wts,
        bb=32, seg=ts, pos0=PAST_LEN, per_token_mod=True)
    st = lambda v: v[None]
    return (y_p, y_s, st(pool_p), st(conv_p), st(lru_p).astype(state_lru.dtype),
            st(pool_s), st(conv_s), st(lru_s).astype(state_lru.dtype))
```

```python
import functools

import jax
import jax.numpy as jnp
from jax import lax
from jax.experimental import pallas as pl
from jax.experimental.pallas import tpu as pltpu

F32 = jnp.float32
BF16 = jnp.bfloat16
I32 = jnp.int32

NORM_EPS = 1e-6
PAST_LEN = 16384
LRU_C = 8.0
POOL_WINDOWS = (2, 4, 8, 16)
POOL_HIST = 16
CONV_WIDTH = 4
CONV_HIST = 8
LRU_HEADS = 8
PEER_HEADS = 8
PEER_NKEYS = 128
PEER_TOPK = 16
SUBLANES = 8
LANES = 128
VMEM_LIMIT = 56 * 1024 * 1024

EXP_TOK = 512
EXP_CHUNK = 2048
EXP_SUB = 256
PAIR_UNROLL = 64
S_PITCH = 136
ROUTE_TOK = 1024
ROUTE_HEADS = 2


def _dot(a, b):
    return jnp.dot(a, b, preferred_element_type=F32)


def _dot_nt(a, b):
    return lax.dot_general(a, b, (((1,), (1,)), ((), ())), preferred_element_type=F32)


def _rmsnorm(x, g):
    return x * lax.rsqrt(jnp.mean(x * x, axis=-1, keepdims=True) + NORM_EPS) * g


def _gelu(x):
    return jax.nn.gelu(x, approximate=True)


def _mod_rows(m_ref, rows):
    m = m_ref[...]
    nseq, _, d = m.shape
    if nseq == 1:
        return m[0]
    return jnp.broadcast_to(m, (nseq, rows // nseq, d)).reshape(rows, d)


def _mod_spec(seq_tokens, tile_tokens, d, col, tile_of):
    if seq_tokens >= tile_tokens:
        tiles_per_seq = seq_tokens // tile_tokens
        return pl.BlockSpec((1, 1, d), lambda *g: (tile_of(*g) // tiles_per_seq, 0, col))
    return pl.BlockSpec((tile_tokens // seq_tokens, 1, d), lambda *g: (tile_of(*g), 0, col))


def _ada_kernel(c_ref, w_ref, b_ref, o_ref):
    c = c_ref[...]
    s = (c * jax.nn.sigmoid(c)).astype(BF16)
    o_ref[...] = _dot(s, w_ref[...].astype(BF16)) + b_ref[...]


def _ada(c, w_ada, b_ada):
    n, d = c.shape
    dout = w_ada.shape[1]
    tn = 1024
    return pl.pallas_call(
        _ada_kernel,
        grid=(dout // tn,),
        in_specs=[
            pl.BlockSpec((n, d), lambda j: (0, 0)),
            pl.BlockSpec((d, tn), lambda j: (0, j)),
            pl.BlockSpec((1, tn), lambda j: (0, j)),
        ],
        out_specs=pl.BlockSpec((n, tn), lambda j: (0, j)),
        out_shape=jax.ShapeDtypeStruct((n, dout), F32),
        compiler_params=pltpu.CompilerParams(
            dimension_semantics=("arbitrary",), vmem_limit_bytes=VMEM_LIMIT),
        name="ada",
    )(c, w_ada, b_ada.reshape(1, dout))


def _mixer_kernel(x_ref, sh_ref, sc_ref, g1_ref, pool0_ref, conv0_ref, h0_ref,
                  norm1_ref, w_in_ref, b_gate_ref, pool_mix_ref, pool_scale_ref,
                  conv_w_ref, conv_b_ref, w_gi_ref, b_rg_ref, b_ig_ref, lam_ref,
                  w_a_ref, w_b_ref, w_out_ref,
                  x1_ref, pool_out_ref, conv_out_ref, h_out_ref,
                  ext_u, ext_c, gate_s, a_s, b_s, hs_s, hcar, proj_s,
                  *, bb, seg, pos0, width):
    rows = bb * seg
    tt = pl.program_id(1)
    w = width
    gdim = w // len(POOL_WINDOWS)
    hdim = w // LRU_HEADS

    @pl.when(tt == 0)
    def _():
        ext_u[:, 0:POOL_HIST, :] = pool0_ref[...]
        ext_c[:, 0:CONV_HIST, :] = conv0_ref[...]
        hcar[...] = h0_ref[...].reshape(bb * SUBLANES, w)

    x = x_ref[...].reshape(rows, w)
    h = _rmsnorm(x, norm1_ref[...]) * (1.0 + _mod_rows(sc_ref, rows)) + _mod_rows(sh_ref, rows)
    hb = h.astype(BF16)

    row = lax.broadcasted_iota(I32, (rows, 1), 0)
    pos = pos0 + tt * seg + (row & (seg - 1))

    proj_s[...] = _dot(hb, w_in_ref[...])
    u = proj_s[:, 0:w]
    ext_u[:, POOL_HIST:POOL_HIST + seg, :] = u.reshape(bb, seg, w)
    ext_rows = bb * (POOL_HIST + seg)
    wsum = ext_u[:, :, 0:w].reshape(ext_rows, w)
    level = 1
    for g, win in enumerate(POOL_WINDOWS):
        lo = g * gdim
        while level < win:
            wsum = wsum + pltpu.roll(wsum, level, 0)
            level *= 2
        acc = wsum[:, 0:gdim].reshape(bb, POOL_HIST + seg, gdim)[:, POOL_HIST:, :].reshape(rows, gdim)
        wsum = wsum[:, gdim:]
        cnt = jnp.minimum(pos + 1, win).astype(F32)
        pooled = acc / cnt - u[:, lo:lo + gdim]
        zg = _dot(pooled.astype(BF16), pool_mix_ref[g])
        gate_s[:, lo:lo + gdim] = zg
    z = gate_s[...] * pool_scale_ref[...]
    y_a = _dot(z.astype(BF16), w_a_ref[...])
    pool_out_ref[...] = ext_u[:, seg:seg + POOL_HIST, :]
    ext_u[:, 0:POOL_HIST, :] = ext_u[:, seg:seg + POOL_HIST, :]

    g_a = jax.nn.sigmoid(proj_s[:, 3 * w:4 * w] + b_gate_ref[:, 0:w])
    m = g_a * y_a

    xb = proj_s[:, w:2 * w]
    ext_c[:, CONV_HIST:CONV_HIST + seg, :] = xb.reshape(bb, seg, w)
    xc = conv_b_ref[...] + xb * conv_w_ref[CONV_WIDTH - 1:CONV_WIDTH, :]
    for k in range(CONV_WIDTH - 1):
        sh = CONV_WIDTH - 1 - k
        xc = xc + ext_c[:, CONV_HIST - sh:CONV_HIST - sh + seg, :].reshape(rows, w) * conv_w_ref[k:k + 1, :]
    conv_out_ref[...] = ext_c[:, seg:seg + CONV_HIST, :]
    ext_c[:, 0:CONV_HIST, :] = ext_c[:, seg:seg + CONV_HIST, :]

    xcb = xc.astype(BF16)
    lam = -lam_ref[...]
    softplus = jnp.maximum(lam, 0.0) + jnp.log1p(jnp.exp(-jnp.abs(lam)))
    for hh in range(LRU_HEADS):
        lo = hh * hdim
        ri = _dot(xcb[:, lo:lo + hdim], w_gi_ref[hh])
        r = jax.nn.sigmoid(ri[:, 0:hdim] + b_rg_ref[:, lo:lo + hdim])
        i = jax.nn.sigmoid(ri[:, hdim:2 * hdim] + b_ig_ref[:, lo:lo + hdim])
        log_a = (-LRU_C) * r * softplus[:, lo:lo + hdim]
        a = jnp.exp(log_a)
        mult = jnp.sqrt(-jnp.tanh(log_a) * (a * a + 1.0))
        mult = jnp.where(pos == 0, 1.0, mult)
        a_s[:, lo:lo + hdim] = a
        b_s[:, lo:lo + hdim] = mult * i * xc[:, lo:lo + hdim]

    tiles = rows // SUBLANES
    a = a_s[...].reshape(tiles, SUBLANES, w)
    b = b_s[...].reshape(tiles, SUBLANES, w)
    rin = lax.broadcasted_iota(I32, (1, SUBLANES, 1), 1)
    for d in (1, 2, 4):
        a_sh = jnp.where(rin >= d, pltpu.roll(a, d, 1), 1.0)
        b_sh = jnp.where(rin >= d, pltpu.roll(b, d, 1), 0.0)
        b = a * b_sh + b
        a = a * a_sh
    a = a.reshape(rows, w)
    b = b.reshape(rows, w)
    if seg == SUBLANES:
        hs = a * hcar[...] + b
        h_out_ref[...] = hs.reshape(bb, SUBLANES, w)
    else:
        a_s[...] = a
        b_s[...] = b

        def tile_step(k, hp):
            r0 = pl.multiple_of(k * SUBLANES, SUBLANES)
            ht = a_s[pl.ds(r0, SUBLANES), :] * hp + b_s[pl.ds(r0, SUBLANES), :]
            hs_s[pl.ds(r0, SUBLANES), :] = ht
            return jnp.broadcast_to(ht[SUBLANES - 1:SUBLANES, :], (SUBLANES, w))

        hp = lax.fori_loop(0, rows // SUBLANES, tile_step, hcar[...])
        hcar[...] = hp
        h_out_ref[...] = hp.reshape(bb, SUBLANES, w)
        hs = hs_s[...]

    yb = proj_s[:, 2 * w:3 * w]
    y_b = _dot((hs * _gelu(yb)).astype(BF16), w_b_ref[...])
    g_b = jax.nn.sigmoid(proj_s[:, 4 * w:5 * w] + b_gate_ref[:, w:2 * w])
    m = m + g_b * y_b
    out = _dot(m.astype(BF16), w_out_ref[...])
    x1_ref[...] = (x + _mod_rows(g1_ref, rows) * out).reshape(bb, seg, w)


def _const_spec(shape):
    nd = len(shape)
    return pl.BlockSpec(shape, lambda *_: (0,) * nd, pipeline_mode=pl.Buffered(1))


def _mixer(x, mod3, pool0, conv0, h0e, weights, *, bb, seg, pos0):
    bsz, t, d = x.shape
    w = d
    rows = bb * seg
    nb, nt = bsz // bb, t // seg
    assert bb == 1 or nt == 1, "a tile is part of one sequence or holds whole sequences"
    mod_specs = [pl.BlockSpec((bb, 1, d), functools.partial(lambda b, tt, col: (b, 0, col), col=col))
                 for col in range(3)]
    wnames = ("norm1", "w_in", "b_gate", "pool_mix", "pool_scale", "conv_w", "conv_b",
              "w_gi", "b_rg", "b_ig", "lam", "w_a", "w_b", "w_out")
    wvals = [weights[k] for k in wnames]
    kern = functools.partial(_mixer_kernel, bb=bb, seg=seg, pos0=pos0, width=w)
    return pl.pallas_call(
        kern,
        grid=(nb, nt),
        in_specs=[
            pl.BlockSpec((bb, seg, d), lambda b, tt: (b, tt, 0)),
            *mod_specs,
            pl.BlockSpec((bb, POOL_HIST, w), lambda b, tt: (b, 0, 0)),
            pl.BlockSpec((bb, CONV_HIST, w), lambda b, tt: (b, 0, 0)),
            pl.BlockSpec((bb, SUBLANES, w), lambda b, tt: (b, 0, 0)),
        ] + [_const_spec(v.shape) for v in wvals],
        out_specs=[
            pl.BlockSpec((bb, seg, d), lambda b, tt: (b, tt, 0)),
            pl.BlockSpec((bb, POOL_HIST, w), lambda b, tt: (b, 0, 0)),
            pl.BlockSpec((bb, CONV_HIST, w), lambda b, tt: (b, 0, 0)),
            pl.BlockSpec((bb, SUBLANES, w), lambda b, tt: (b, 0, 0)),
        ],
        out_shape=[
            jax.ShapeDtypeStruct((bsz, t, d), F32),
            jax.ShapeDtypeStruct((bsz, POOL_HIST, w), F32),
            jax.ShapeDtypeStruct((bsz, CONV_HIST, w), F32),
            jax.ShapeDtypeStruct((bsz, SUBLANES, w), F32),
        ],
        scratch_shapes=[
            pltpu.VMEM((bb, POOL_HIST + seg, w), F32),
            pltpu.VMEM((bb, CONV_HIST + seg, w), F32),
            pltpu.VMEM((rows, w), F32),
            pltpu.VMEM((rows, w), F32),
            pltpu.VMEM((rows, w), F32),
            pltpu.VMEM((rows, w), F32),
            pltpu.VMEM((bb * SUBLANES, w), F32),
            pltpu.VMEM((rows, weights["w_in"].shape[1]), F32),
        ],
        compiler_params=pltpu.CompilerParams(
            dimension_semantics=("arbitrary", "arbitrary"), vmem_limit_bytes=VMEM_LIMIT),
        name="mixer",
    )(x, mod3, mod3, mod3, pool0, conv0, h0e, *wvals)


def _sort16_network():
    n, pairs, p = 16, [], 1
    while p < n:
        k = p
        while k >= 1:
            for j in range(k % p, n - k, 2 * k):
                for i in range(min(k, n - j - k)):
                    if (i + j) // (2 * p) == (i + j + k) // (2 * p):
                        pairs.append((i + j, i + j + k))
            k //= 2
        p *= 2
    return pairs


def _top16_rows(s, val_ref, idx_ref):
    nt = s.shape[0] // SUBLANES
    assert nt == PEER_TOPK
    sub = lax.broadcasted_iota(I32, (SUBLANES, LANES), 0).astype(F32)
    v = [s[SUBLANES * k:SUBLANES * (k + 1), :] for k in range(nt)]
    ix = [sub + float(SUBLANES * k) for k in range(nt)]
    for i, j in _sort16_network():
        swap = (v[j] > v[i]) | ((v[j] == v[i]) & (ix[j] < ix[i]))
        v[i], v[j] = jnp.where(swap, v[j], v[i]), jnp.where(swap, v[i], v[j])
        ix[i], ix[j] = jnp.where(swap, ix[j], ix[i]), jnp.where(swap, ix[i], ix[j])
    for it in range(PEER_TOPK):
        mx = jnp.max(v[0], axis=0, keepdims=True)
        best = jnp.min(jnp.where(v[0] == mx, ix[0], float(s.shape[0])), axis=0, keepdims=True)
        val_ref[it:it + 1, :] = mx
        idx_ref[it:it + 1, :] = best
        win = ix[0] == best
        for r in range(PEER_TOPK - 1 - it):
            v[r] = jnp.where(win, v[r + 1], v[r])
            ix[r] = jnp.where(win, ix[r + 1], ix[r])


def _route_kernel(x_ref, sh_ref, sc_ref, norm2_ref, wq_ref, keys_ref,
                  h2_ref, e_ref, g_ref, h2_s, q_s, *list_refs, ntok):
    hd = pl.program_id(1)
    nsub = ntok // LANES
    lists = [list_refs[6 * u:6 * u + 6] for u in range(ROUTE_HEADS * nsub)]

    @pl.when(hd == 0)
    def _():
        x = x_ref[...]
        h2 = _rmsnorm(x, norm2_ref[...]) * (1.0 + _mod_rows(sc_ref, ntok)) + _mod_rows(sh_ref, ntok)
        h2b = h2.astype(BF16)
        h2_s[...] = h2b
        h2_ref[...] = h2b

    q_s[...] = _dot(h2_s[...], wq_ref[...])
    half = PEER_NKEYS

    sub = lax.broadcasted_iota(I32, (SUBLANES, LANES), 0)
    is_row = sub < 4
    nkf = float(PEER_NKEYS)
    nexp = nkf * nkf
    depth = jnp.full((SUBLANES, LANES), 0, I32)
    for r, dep in enumerate((16, 8, 5, 4, 12, 4, 1)):
        depth = jnp.where(sub == r, dep, depth)
    subf = sub.astype(F32)
    flat0 = jnp.where(is_row, subf * PEER_TOPK, 4.0 * PEER_TOPK + (subf - 4.0)) * nexp
    dflat = jnp.where(is_row, 1.0, float(PEER_TOPK)) * nexp

    def sub_block(hh, j):
        v1_s, i1_s, v2_s, i2_s, bs_s, bk_s = lists[hh * nsub + j]
        qj = q_s[j * LANES:(j + 1) * LANES, 2 * half * hh:2 * half * (hh + 1)].astype(BF16)
        _top16_rows(_dot_nt(keys_ref[hh, 0], qj[:, 0:half]), v1_s, i1_s)
        _top16_rows(_dot_nt(keys_ref[hh, 1], qj[:, half:2 * half]), v2_s, i2_s)
        vfix = jnp.where(is_row, v1_s[0:SUBLANES, :], pltpu.roll(v2_s[0:SUBLANES, :], 4, 0))
        efix = jnp.where(is_row, i1_s[0:SUBLANES, :] * nkf, pltpu.roll(i2_s[0:SUBLANES, :], 4, 0))
        t, k = [], []
        for d in range(PEER_TOPK):
            if d + 4 < PEER_TOPK:
                vrun = jnp.where(is_row, v2_s[d:d + 1, :], v1_s[d + 4:d + 5, :])
                erun = jnp.where(is_row, i2_s[d:d + 1, :], i1_s[d + 4:d + 5, :] * nkf)
            else:
                vrun = v2_s[d:d + 1, :]
                erun = i2_s[d:d + 1, :]
            t.append(jnp.where(depth > d, vfix + vrun, -jnp.inf))
            k.append(flat0 + dflat * float(d) + (efix + erun))
        for it in range(PEER_TOPK):
            mx = jnp.max(t[0], axis=0, keepdims=True)
            kx = jnp.min(jnp.where(t[0] == mx, k[0], PEER_TOPK * PEER_TOPK * nexp), axis=0, keepdims=True)
            bk_s[it:it + 1, :] = kx
            bs_s[it:it + 1, :] = mx
            win = k[0] == kx
            for r in range(PEER_TOPK - 1 - it):
                t[r] = jnp.where(win, t[r + 1], t[r])
                k[r] = jnp.where(win, k[r + 1], k[r])
        best = bs_s[...]
        ex = jnp.exp(best - best[0:1, :])
        gsm = ex / jnp.sum(ex, axis=0, keepdims=True)
        bk = bk_s[...]
        rows = slice(hh * PEER_TOPK, (hh + 1) * PEER_TOPK)
        e_ref[j, rows, :] = bk - jnp.floor(bk / nexp) * nexp
        g_ref[j, rows, :] = gsm

    for hh in range(ROUTE_HEADS):
        for j in range(nsub):
            sub_block(hh, j)


def _route(x1, mod3, norm2, w_q, keys, *, seq_tokens):
    n, d = x1.shape
    nt = ROUTE_TOK
    tiles = n // nt
    mod_specs = [_mod_spec(seq_tokens, nt, d, col, lambda i, hd: i) for col in (3, 4)]
    dk2 = w_q.shape[1] // PEER_HEADS
    rh = ROUTE_HEADS
    kern = functools.partial(_route_kernel, ntok=nt)
    return pl.pallas_call(
        kern,
        grid=(tiles, PEER_HEADS // rh),
        in_specs=[
            pl.BlockSpec((nt, d), lambda i, hd: (i, 0)),
            *mod_specs,
            pl.BlockSpec((1, d), lambda i, hd: (0, 0)),
            pl.BlockSpec((d, rh * dk2), lambda i, hd: (0, hd)),
            pl.BlockSpec((rh, 2, PEER_NKEYS, dk2 // 2), lambda i, hd: (hd, 0, 0, 0)),
        ],
        out_specs=[
            pl.BlockSpec((nt, d), lambda i, hd: (i, 0)),
            pl.BlockSpec((nt // LANES, rh * PEER_TOPK, LANES), lambda i, hd: (i, hd, 0)),
            pl.BlockSpec((nt // LANES, rh * PEER_TOPK, LANES), lambda i, hd: (i, hd, 0)),
        ],
        out_shape=[
            jax.ShapeDtypeStruct((n, d), BF16),
            jax.ShapeDtypeStruct((n // LANES, PEER_HEADS * PEER_TOPK, LANES), F32),
            jax.ShapeDtypeStruct((n // LANES, PEER_HEADS * PEER_TOPK, LANES), F32),
        ],
        scratch_shapes=[
            pltpu.VMEM((nt, d), BF16),
            pltpu.VMEM((nt, rh * dk2), F32),
        ] + [pltpu.VMEM((PEER_TOPK, LANES), F32)] * (6 * rh * (nt // LANES)),
        compiler_params=pltpu.CompilerParams(
            dimension_semantics=("arbitrary", "arbitrary"), vmem_limit_bytes=VMEM_LIMIT),
        name="route",
    )(x1, mod3, mod3, norm2, w_q, keys)


def _expert_kernel(h2_ref, e_ref, g_ref, u_ref, v_ref, x1_ref, g2_ref, fnorm_ref,
                   y_ref, s_s, erow_s, grow_s, w2_s, acc_s, *, ntok):
    c = pl.program_id(1)
    nchunks = pl.num_programs(1)
    nk = PEER_NKEYS

    @pl.when(c == 0)
    def _():
        acc_s[...] = jnp.zeros_like(acc_s)
        for sb in range(ntok // LANES):
            erow_s[sb * LANES:(sb + 1) * LANES, :] = e_ref[sb].T.astype(I32)
            grow_s[sb * LANES:(sb + 1) * LANES, :] = g_ref[sb].T
        riota = lax.broadcasted_iota(I32, (nk, nk), 0)
        riota2 = lax.broadcasted_iota(I32, (2 * nk, nk), 0)

        def pair(p):
            n0 = pl.multiple_of(2 * p, 2)
            lts, rts = [], []
            for t in range(2):
                e = erow_s[pl.ds(n0 + t, 1), :]
                gv = grow_s[pl.ds(n0 + t, 1), :]
                i1 = e >> 7
                i2 = e & (nk - 1)
                lts.append(jnp.where(riota2 == 2 * i1 + t, gv, 0.0).astype(BF16))
                rts.append(jnp.where(riota == i2, 1.0, 0.0).astype(BF16))
            x = _dot_nt(jnp.concatenate(lts, axis=1), jnp.concatenate(rts, axis=1))
            words = pltpu.bitcast(x.astype(BF16), I32)
            s_s[pl.ds(pl.multiple_of(p * S_PITCH, SUBLANES), nk), :] = words

        def pair_group(pg, carry):
            for q in range(PAIR_UNROLL):
                pair(pg * PAIR_UNROLL + q)
            return carry

        lax.fori_loop(0, ntok // (2 * PAIR_UNROLL), pair_group, 0)

    h2 = h2_ref[...]
    i1_per_sub = EXP_SUB // nk
    for j in range(EXP_CHUNK // EXP_SUB):
        a = _dot_nt(h2, u_ref[j * EXP_SUB:(j + 1) * EXP_SUB, :])
        i1_0 = c * (EXP_CHUNK // nk) + j * i1_per_sub
        gs = [pltpu.bitcast(s_s[pl.ds(i1_0 + t, ntok // 2, stride=S_PITCH), :], BF16)
              for t in range(i1_per_sub)]
        g = jnp.concatenate(gs, axis=1).astype(F32)
        w2_s[:, j * EXP_SUB:(j + 1) * EXP_SUB] = (_gelu(a) * g).astype(BF16)
    acc_s[...] += _dot(w2_s[...], v_ref[...])

    @pl.when(c == nchunks - 1)
    def _():
        x2 = x1_ref[...] + _mod_rows(g2_ref, ntok) * acc_s[...]
        y_ref[...] = _rmsnorm(x2, fnorm_ref[...])


def _expert(h2, eT, gT, u_bf, v_bf, x1, mod3, fnorm, *, seq_tokens):
    n, d = x1.shape
    nt = EXP_TOK
    ne = u_bf.shape[0]
    kern = functools.partial(_expert_kernel, ntok=nt)
    return pl.pallas_call(
        kern,
        grid=(n // nt, ne // EXP_CHUNK),
        in_specs=[
            pl.BlockSpec((nt, d), lambda i, c: (i, 0)),
            pl.BlockSpec((nt // LANES, PEER_HEADS * PEER_TOPK, LANES), lambda i, c: (i, 0, 0)),
            pl.BlockSpec((nt // LANES, PEER_HEADS * PEER_TOPK, LANES), lambda i, c: (i, 0, 0)),
            pl.BlockSpec((EXP_CHUNK, d), lambda i, c: (c, 0)),
            pl.BlockSpec((EXP_CHUNK, d), lambda i, c: (c, 0)),
            pl.BlockSpec((nt, d), lambda i, c: (i, 0)),
            _mod_spec(seq_tokens, nt, d, 5, lambda i, c: i),
            pl.BlockSpec((1, d), lambda i, c: (0, 0)),
        ],
        out_specs=pl.BlockSpec((nt, d), lambda i, c: (i, 0)),
        out_shape=jax.ShapeDtypeStruct((n, d), F32),
        scratch_shapes=[
            pltpu.VMEM((nt // 2 * S_PITCH, LANES), I32),
            pltpu.VMEM((nt, PEER_HEADS * PEER_TOPK), I32),
            pltpu.VMEM((nt, PEER_HEADS * PEER_TOPK), F32),
            pltpu.VMEM((nt, EXP_CHUNK), BF16),
            pltpu.VMEM((nt, d), F32),
        ],
        compiler_params=pltpu.CompilerParams(
            dimension_semantics=("arbitrary", "arbitrary"), vmem_limit_bytes=VMEM_LIMIT),
        name="expert",
    )(h2, eT, gT, u_bf, v_bf, x1, mod3, fnorm)


def _run_group(x, mod, pool0, conv0, h0, wts, *, bb, seg, pos0):
    bsz, t, d = x.shape
    n = bsz * t
    mod3 = mod[:, None, :]
    pool0p = jnp.pad(pool0, ((0, 0), (POOL_HIST - pool0.shape[1], 0), (0, 0)))
    conv0p = jnp.pad(conv0, ((0, 0), (CONV_HIST - conv0.shape[1], 0), (0, 0)))
    h0e = jnp.broadcast_to(h0[:, None, :], (bsz, SUBLANES, h0.shape[-1]))
    x1, pool_o, conv_o, h_o = _mixer(x, mod3, pool0p, conv0p, h0e, wts, bb=bb, seg=seg, pos0=pos0)
    x1f = x1.reshape(n, d)
    h2, eT, gT = _route(x1f, mod3, wts["norm2"], wts["w_q"], wts["keys"], seq_tokens=t)
    y = _expert(h2, eT, gT, wts["u"], wts["v"], x1f, mod3, wts["fnorm"], seq_tokens=t)
    return (y.reshape(bsz, t, d), pool_o[:, 1:], conv_o[:, CONV_HIST - (CONV_WIDTH - 1):],
            h_o[:, SUBLANES - 1])


def kernel(x_prompt, x_sample, c_prompt, c_sample, state_pool, state_conv, state_lru, norm1, norm2, w_ada, b_ada, w_in, b_gate, pool_mix, pool_scale, conv_w, conv_b, w_rg, b_rg, w_ig, b_ig, lru_lambda, w_branch_a, w_branch_b, w_out, w_q, sub_keys, expert_u, expert_v, final_norm):
    depth = w_in.shape[0]
    assert depth == 1, "single-layer trunk"
    bp, tp, d = x_prompt.shape
    bs, ts, _ = x_sample.shape
    l = 0
    mod = _ada(jnp.concatenate([c_prompt, c_sample], axis=0), w_ada[l], b_ada[l])
    row = lambda v: v.reshape(1, -1)
    wts = {
        "norm1": row(norm1[l]), "norm2": row(norm2[l]), "fnorm": row(final_norm),
        "w_in": w_in[l].astype(BF16), "b_gate": row(b_gate[l]),
        "pool_mix": pool_mix[l].astype(BF16), "pool_scale": row(pool_scale[l]),
        "conv_w": conv_w[l], "conv_b": row(conv_b[l]),
        "w_gi": jnp.concatenate([w_rg[l], w_ig[l]], axis=-1).astype(BF16),
        "b_rg": row(b_rg[l]), "b_ig": row(b_ig[l]), "lam": row(lru_lambda[l]),
        "w_a": w_branch_a[l].astype(BF16), "w_b": w_branch_b[l].astype(BF16),
        "w_out": w_out[l].astype(BF16), "w_q": w_q[l].astype(BF16),
        "keys": sub_keys[l].astype(BF16),
        "u": expert_u[l].astype(BF16), "v": expert_v[l].astype(BF16),
    }
    zp = jnp.zeros((bp, state_pool.shape[2], state_pool.shape[3]), F32)
    zc = jnp.zeros((bp, state_conv.shape[2], state_conv.shape[3]), F32)
    zh = jnp.zeros((bp, state_lru.shape[2]), F32)
    y_p, pool_p, conv_p, lru_p = _run_group(
        x_prompt, mod[:bp], zp, zc, zh, wts, bb=1, seg=256, pos0=0)
    y_s, pool_s, conv_s, lru_s = _run_group(
        x_sample, mod[bp:], state_pool[l], state_conv[l], state_lru[l], wts,
        bb=32, seg=ts, pos0=PAST_LEN)
    st = lambda v: v[None]
    return (y_p, y_s, st(pool_p), st(conv_p), st(lru_p).astype(state_lru.dtype),
            st(pool_s), st(conv_s), st(lru_s).astype(state_lru.dtype))
```

```python
import functools

import jax
import jax.numpy as jnp
from jax import lax
from jax.experimental import pallas as pl
from jax.experimental.pallas import tpu as pltpu

F32 = jnp.float32
BF16 = jnp.bfloat16
I32 = jnp.int32

NORM_EPS = 1e-6
PAST_LEN = 16384
LRU_C = 8.0
POOL_WINDOWS = (2, 4, 8, 16)
POOL_ROWS = max(POOL_WINDOWS) - 1
POOL_HIST = 16
CONV_WIDTH = 4
CONV_ROWS = CONV_WIDTH - 1
CONV_HIST = 8
LRU_HEADS = 8
PEER_HEADS = 8
PEER_NKEYS = 128
PEER_TOPK = 16
SUBLANES = 8
LANES = 128
VMEM_LIMIT = 56 * 1024 * 1024

EXP_TOK = 512
EXP_CHUNK = 2048
EXP_SUB = 256
PAIR_UNROLL = 64
S_PITCH = 136
ROUTE_TOK = 1024
ROUTE_HEADS = 2


def _dot(a, b):
    return jnp.dot(a, b, preferred_element_type=F32)


def _dot_nt(a, b):
    return lax.dot_general(a, b, (((1,), (1,)), ((), ())), preferred_element_type=F32)


def _rmsnorm(x, g):
    return x * lax.rsqrt(jnp.mean(x * x, axis=-1, keepdims=True) + NORM_EPS) * g


def _gelu(x):
    return jax.nn.gelu(x, approximate=True)


def _mod_rows(m_ref, rows):
    m = m_ref[...]
    nseq, _, d = m.shape
    if nseq == 1:
        return m[0]
    return jnp.broadcast_to(m, (nseq, rows // nseq, d)).reshape(rows, d)


def _mod_spec(seq_tokens, tile_tokens, d, col, tile_of):
    if seq_tokens >= tile_tokens:
        tiles_per_seq = seq_tokens // tile_tokens
        return pl.BlockSpec((1, 1, d), lambda *g: (tile_of(*g) // tiles_per_seq, 0, col))
    return pl.BlockSpec((tile_tokens // seq_tokens, 1, d), lambda *g: (tile_of(*g), 0, col))


def _ada_kernel(c_ref, w_ref, b_ref, o_ref):
    c = c_ref[...]
    s = (c * jax.nn.sigmoid(c)).astype(BF16)
    o_ref[...] = _dot(s, w_ref[...].astype(BF16)) + b_ref[...]


def _ada(c, w_ada, b_ada):
    n, d = c.shape
    dout = w_ada.shape[1]
    tn = 1024
    return pl.pallas_call(
        _ada_kernel,
        grid=(dout // tn,),
        in_specs=[
            pl.BlockSpec((n, d), lambda j: (0, 0)),
            pl.BlockSpec((d, tn), lambda j: (0, j)),
            pl.BlockSpec((1, tn), lambda j: (0, j)),
        ],
        out_specs=pl.BlockSpec((n, tn), lambda j: (0, j)),
        out_shape=jax.ShapeDtypeStruct((n, dout), F32),
        compiler_params=pltpu.CompilerParams(
            dimension_semantics=("arbitrary",), vmem_limit_bytes=VMEM_LIMIT),
        name="ada",
    )(c, w_ada, b_ada.reshape(1, dout))


def _mixer_kernel(x_ref, sh_ref, sc_ref, g1_ref, pool0_ref, conv0_ref, h0_ref,
                  norm1_ref, w_in_ref, b_gate_ref, pool_mix_ref, pool_scale_ref,
                  conv_w_ref, conv_b_ref, w_gi_ref, b_rg_ref, b_ig_ref, lam_ref,
                  w_a_ref, w_b_ref, w_out_ref,
                  x1_ref, pool_out_ref, conv_out_ref, h_out_ref,
                  ext_u, ext_c, gate_s, a_s, b_s, hs_s, hcar, proj_s,
                  *, bb, seg, pos0, width):
    rows = bb * seg
    tt = pl.program_id(1)
    w = width
    gdim = w // len(POOL_WINDOWS)
    hdim = w // LRU_HEADS

    @pl.when(tt == 0)
    def _():
        ext_u[:, 0:POOL_HIST, :] = jnp.zeros((bb, POOL_HIST, w), F32)
        ext_u[:, POOL_HIST - POOL_ROWS:POOL_HIST, :] = pool0_ref[...]
        ext_c[:, 0:CONV_HIST, :] = jnp.zeros((bb, CONV_HIST, w), F32)
        ext_c[:, CONV_HIST - CONV_ROWS:CONV_HIST, :] = conv0_ref[...]
        hcar[...] = jnp.broadcast_to(h0_ref[...], (bb, SUBLANES, w)).reshape(bb * SUBLANES, w)

    x = x_ref[...].reshape(rows, w)
    h = _rmsnorm(x, norm1_ref[...]) * (1.0 + _mod_rows(sc_ref, rows)) + _mod_rows(sh_ref, rows)
    hb = h.astype(BF16)

    row = lax.broadcasted_iota(I32, (rows, 1), 0)
    pos = pos0 + tt * seg + (row & (seg - 1))

    proj_s[...] = _dot(hb, w_in_ref[...])
    u = proj_s[:, 0:w]
    ext_u[:, POOL_HIST:POOL_HIST + seg, :] = u.reshape(bb, seg, w)
    ext_rows = bb * (POOL_HIST + seg)
    wsum = ext_u[:, :, 0:w].reshape(ext_rows, w)
    level = 1
    for g, win in enumerate(POOL_WINDOWS):
        lo = g * gdim
        while level < win:
            wsum = wsum + pltpu.roll(wsum, level, 0)
            level *= 2
        acc = wsum[:, 0:gdim].reshape(bb, POOL_HIST + seg, gdim)[:, POOL_HIST:, :].reshape(rows, gdim)
        wsum = wsum[:, gdim:]
        cnt = jnp.minimum(pos + 1, win).astype(F32)
        pooled = acc / cnt - u[:, lo:lo + gdim]
        zg = _dot(pooled.astype(BF16), pool_mix_ref[g])
        gate_s[:, lo:lo + gdim] = zg
    z = gate_s[...] * pool_scale_ref[...]
    y_a = _dot(z.astype(BF16), w_a_ref[...])
    pool_out_ref[...] = ext_u[:, seg + POOL_HIST - POOL_ROWS:seg + POOL_HIST, :]
    ext_u[:, 0:POOL_HIST, :] = ext_u[:, seg:seg + POOL_HIST, :]

    g_a = jax.nn.sigmoid(proj_s[:, 3 * w:4 * w] + b_gate_ref[:, 0:w])
    m = g_a * y_a

    xb = proj_s[:, w:2 * w]
    ext_c[:, CONV_HIST:CONV_HIST + seg, :] = xb.reshape(bb, seg, w)
    xc = conv_b_ref[...] + xb * conv_w_ref[CONV_WIDTH - 1:CONV_WIDTH, :]
    for k in range(CONV_WIDTH - 1):
        sh = CONV_WIDTH - 1 - k
        xc = xc + ext_c[:, CONV_HIST - sh:CONV_HIST - sh + seg, :].reshape(rows, w) * conv_w_ref[k:k + 1, :]
    conv_out_ref[...] = ext_c[:, seg + CONV_HIST - CONV_ROWS:seg + CONV_HIST, :]
    ext_c[:, 0:CONV_HIST, :] = ext_c[:, seg:seg + CONV_HIST, :]

    xcb = xc.astype(BF16)
    lam = -lam_ref[...]
    softplus = jnp.maximum(lam, 0.0) + jnp.log1p(jnp.exp(-jnp.abs(lam)))
    for hh in range(LRU_HEADS):
        lo = hh * hdim
        ri = _dot(xcb[:, lo:lo + hdim], w_gi_ref[hh])
        r = jax.nn.sigmoid(ri[:, 0:hdim] + b_rg_ref[:, lo:lo + hdim])
        i = jax.nn.sigmoid(ri[:, hdim:2 * hdim] + b_ig_ref[:, lo:lo + hdim])
        log_a = (-LRU_C) * r * softplus[:, lo:lo + hdim]
        a = jnp.exp(log_a)
        mult = jnp.sqrt(-jnp.tanh(log_a) * (a * a + 1.0))
        mult = jnp.where(pos == 0, 1.0, mult)
        a_s[:, lo:lo + hdim] = a
        b_s[:, lo:lo + hdim] = mult * i * xc[:, lo:lo + hdim]

    tiles = rows // SUBLANES
    a = a_s[...].reshape(tiles, SUBLANES, w)
    b = b_s[...].reshape(tiles, SUBLANES, w)
    rin = lax.broadcasted_iota(I32, (1, SUBLANES, 1), 1)
    for d in (1, 2, 4):
        a_sh = jnp.where(rin >= d, pltpu.roll(a, d, 1), 1.0)
        b_sh = jnp.where(rin >= d, pltpu.roll(b, d, 1), 0.0)
        b = a * b_sh + b
        a = a * a_sh
    a = a.reshape(rows, w)
    b = b.reshape(rows, w)
    if seg == SUBLANES:
        hs = a * hcar[...] + b
        h_out_ref[...] = hs.reshape(bb, SUBLANES, w)[:, SUBLANES - 1:SUBLANES, :]
    else:
        a_s[...] = a
        b_s[...] = b

        def tile_step(k, hp):
            r0 = pl.multiple_of(k * SUBLANES, SUBLANES)
            ht = a_s[pl.ds(r0, SUBLANES), :] * hp + b_s[pl.ds(r0, SUBLANES), :]
            hs_s[pl.ds(r0, SUBLANES), :] = ht
            return jnp.broadcast_to(ht[SUBLANES - 1:SUBLANES, :], (SUBLANES, w))

        hp = lax.fori_loop(0, rows // SUBLANES, tile_step, hcar[...])
        hcar[...] = hp
        h_out_ref[...] = hp.reshape(bb, SUBLANES, w)[:, SUBLANES - 1:SUBLANES, :]
        hs = hs_s[...]

    yb = proj_s[:, 2 * w:3 * w]
    y_b = _dot((hs * _gelu(yb)).astype(BF16), w_b_ref[...])
    g_b = jax.nn.sigmoid(proj_s[:, 4 * w:5 * w] + b_gate_ref[:, w:2 * w])
    m = m + g_b * y_b
    out = _dot(m.astype(BF16), w_out_ref[...])
    x1_ref[...] = (x + _mod_rows(g1_ref, rows) * out).reshape(bb, seg, w)


def _const_spec(shape):
    nd = len(shape)
    return pl.BlockSpec(shape, lambda *_: (0,) * nd, pipeline_mode=pl.Buffered(1))


def _mixer(x, mod3, pool0, conv0, h0e, weights, *, bb, seg, pos0):
    bsz, t, d = x.shape
    w = d
    rows = bb * seg
    nb, nt = bsz // bb, t // seg
    assert bb == 1 or nt == 1, "a tile is part of one sequence or holds whole sequences"
    mod_specs = [pl.BlockSpec((bb, 1, d), functools.partial(lambda b, tt, col: (b, 0, col), col=col))
                 for col in range(3)]
    wnames = ("norm1", "w_in", "b_gate", "pool_mix", "pool_scale", "conv_w", "conv_b",
              "w_gi", "b_rg", "b_ig", "lam", "w_a", "w_b", "w_out")
    wvals = [weights[k] for k in wnames]
    kern = functools.partial(_mixer_kernel, bb=bb, seg=seg, pos0=pos0, width=w)
    return pl.pallas_call(
        kern,
        grid=(nb, nt),
        in_specs=[
            pl.BlockSpec((bb, seg, d), lambda b, tt: (b, tt, 0)),
            *mod_specs,
            pl.BlockSpec((bb, POOL_ROWS, w), lambda b, tt: (b, 0, 0)),
            pl.BlockSpec((bb, CONV_ROWS, w), lambda b, tt: (b, 0, 0)),
            pl.BlockSpec((bb, 1, w), lambda b, tt: (b, 0, 0)),
        ] + [_const_spec(v.shape) for v in wvals],
        out_specs=[
            pl.BlockSpec((bb, seg, d), lambda b, tt: (b, tt, 0)),
            pl.BlockSpec((bb, POOL_ROWS, w), lambda b, tt: (b, 0, 0)),
            pl.BlockSpec((bb, CONV_ROWS, w), lambda b, tt: (b, 0, 0)),
            pl.BlockSpec((bb, 1, w), lambda b, tt: (b, 0, 0)),
        ],
        out_shape=[
            jax.ShapeDtypeStruct((bsz, t, d), F32),
            jax.ShapeDtypeStruct((bsz, POOL_ROWS, w), F32),
            jax.ShapeDtypeStruct((bsz, CONV_ROWS, w), F32),
            jax.ShapeDtypeStruct((bsz, 1, w), F32),
        ],
        scratch_shapes=[
            pltpu.VMEM((bb, POOL_HIST + seg, w), F32),
            pltpu.VMEM((bb, CONV_HIST + seg, w), F32),
            pltpu.VMEM((rows, w), F32),
            pltpu.VMEM((rows, w), F32),
            pltpu.VMEM((rows, w), F32),
            pltpu.VMEM((rows, w), F32),
            pltpu.VMEM((bb * SUBLANES, w), F32),
            pltpu.VMEM((rows, weights["w_in"].shape[1]), F32),
        ],
        compiler_params=pltpu.CompilerParams(
            dimension_semantics=("arbitrary", "arbitrary"), vmem_limit_bytes=VMEM_LIMIT),
        name="mixer",
    )(x, mod3, mod3, mod3, pool0, conv0, h0e, *wvals)


def _sort16_network():
    n, pairs, p = 16, [], 1
    while p < n:
        k = p
        while k >= 1:
            for j in range(k % p, n - k, 2 * k):
                for i in range(min(k, n - j - k)):
                    if (i + j) // (2 * p) == (i + j + k) // (2 * p):
                        pairs.append((i + j, i + j + k))
            k //= 2
        p *= 2
    return pairs


def _top16_rows(s, val_ref, idx_ref):
    nt = s.shape[0] // SUBLANES
    assert nt == PEER_TOPK
    sub = lax.broadcasted_iota(I32, (SUBLANES, LANES), 0).astype(F32)
    v = [s[SUBLANES * k:SUBLANES * (k + 1), :] for k in range(nt)]
    ix = [sub + float(SUBLANES * k) for k in range(nt)]
    for i, j in _sort16_network():
        swap = (v[j] > v[i]) | ((v[j] == v[i]) & (ix[j] < ix[i]))
        v[i], v[j] = jnp.where(swap, v[j], v[i]), jnp.where(swap, v[i], v[j])
        ix[i], ix[j] = jnp.where(swap, ix[j], ix[i]), jnp.where(swap, ix[i], ix[j])
    for it in range(PEER_TOPK):
        mx = jnp.max(v[0], axis=0, keepdims=True)
        best = jnp.min(jnp.where(v[0] == mx, ix[0], float(s.shape[0])), axis=0, keepdims=True)
        val_ref[it:it + 1, :] = mx
        idx_ref[it:it + 1, :] = best
        win = ix[0] == best
        for r in range(PEER_TOPK - 1 - it):
            v[r] = jnp.where(win, v[r + 1], v[r])
            ix[r] = jnp.where(win, ix[r + 1], ix[r])


def _route_kernel(x_ref, sh_ref, sc_ref, norm2_ref, wq_ref, keys_ref,
                  h2_ref, e_ref, g_ref, h2_s, q_s, *list_refs, ntok):
    hd = pl.program_id(1)
    nsub = ntok // LANES
    lists = [list_refs[6 * u:6 * u + 6] for u in range(ROUTE_HEADS * nsub)]

    @pl.when(hd == 0)
    def _():
        x = x_ref[...]
        h2 = _rmsnorm(x, norm2_ref[...]) * (1.0 + _mod_rows(sc_ref, ntok)) + _mod_rows(sh_ref, ntok)
        h2b = h2.astype(BF16)
        h2_s[...] = h2b
        h2_ref[...] = h2b

    q_s[...] = _dot(h2_s[...], wq_ref[...])
    half = PEER_NKEYS

    sub = lax.broadcasted_iota(I32, (SUBLANES, LANES), 0)
    is_row = sub < 4
    nkf = float(PEER_NKEYS)
    nexp = nkf * nkf
    depth = jnp.full((SUBLANES, LANES), 0, I32)
    for r, dep in enumerate((16, 8, 5, 4, 12, 4, 1)):
        depth = jnp.where(sub == r, dep, depth)
    subf = sub.astype(F32)
    flat0 = jnp.where(is_row, subf * PEER_TOPK, 4.0 * PEER_TOPK + (subf - 4.0)) * nexp
    dflat = jnp.where(is_row, 1.0, float(PEER_TOPK)) * nexp

    def sub_block(hh, j):
        v1_s, i1_s, v2_s, i2_s, bs_s, bk_s = lists[hh * nsub + j]
        qj = q_s[j * LANES:(j + 1) * LANES, 2 * half * hh:2 * half * (hh + 1)].astype(BF16)
        _top16_rows(_dot_nt(keys_ref[hh, 0], qj[:, 0:half]), v1_s, i1_s)
        _top16_rows(_dot_nt(keys_ref[hh, 1], qj[:, half:2 * half]), v2_s, i2_s)
        vfix = jnp.where(is_row, v1_s[0:SUBLANES, :], pltpu.roll(v2_s[0:SUBLANES, :], 4, 0))
        efix = jnp.where(is_row, i1_s[0:SUBLANES, :] * nkf, pltpu.roll(i2_s[0:SUBLANES, :], 4, 0))
        t, k = [], []
        for d in range(PEER_TOPK):
            if d + 4 < PEER_TOPK:
                vrun = jnp.where(is_row, v2_s[d:d + 1, :], v1_s[d + 4:d + 5, :])
                erun = jnp.where(is_row, i2_s[d:d + 1, :], i1_s[d + 4:d + 5, :] * nkf)
            else:
                vrun = v2_s[d:d + 1, :]
                erun = i2_s[d:d + 1, :]
            t.append(jnp.where(depth > d, vfix + vrun, -jnp.inf))
            k.append(flat0 + dflat * float(d) + (efix + erun))
        for it in range(PEER_TOPK):
            mx = jnp.max(t[0], axis=0, keepdims=True)
            kx = jnp.min(jnp.where(t[0] == mx, k[0], PEER_TOPK * PEER_TOPK * nexp), axis=0, keepdims=True)
            bk_s[it:it + 1, :] = kx
            bs_s[it:it + 1, :] = mx
            win = k[0] == kx
            for r in range(PEER_TOPK - 1 - it):
                t[r] = jnp.where(win, t[r + 1], t[r])
                k[r] = jnp.where(win, k[r + 1], k[r])
        best = bs_s[...]
        ex = jnp.exp(best - best[0:1, :])
        gsm = ex / jnp.sum(ex, axis=0, keepdims=True)
        bk = bk_s[...]
        rows = slice(hh * PEER_TOPK, (hh + 1) * PEER_TOPK)
        e_ref[j, rows, :] = bk - jnp.floor(bk / nexp) * nexp
        g_ref[j, rows, :] = gsm

    for hh in range(ROUTE_HEADS):
        for j in range(nsub):
            sub_block(hh, j)


def _route(x1, mod3, norm2, w_q, keys, *, seq_tokens):
    n, d = x1.shape
    nt = ROUTE_TOK
    tiles = n // nt
    mod_specs = [_mod_spec(seq_tokens, nt, d, col, lambda i, hd: i) for col in (3, 4)]
    dk2 = w_q.shape[1] // PEER_HEADS
    rh = ROUTE_HEADS
    kern = functools.partial(_route_kernel, ntok=nt)
    return pl.pallas_call(
        kern,
        grid=(tiles, PEER_HEADS // rh),
        in_specs=[
            pl.BlockSpec((nt, d), lambda i, hd: (i, 0)),
            *mod_specs,
            pl.BlockSpec((1, d), lambda i, hd: (0, 0)),
            pl.BlockSpec((d, rh * dk2), lambda i, hd: (0, hd)),
            pl.BlockSpec((rh, 2, PEER_NKEYS, dk2 // 2), lambda i, hd: (hd, 0, 0, 0)),
        ],
        out_specs=[
            pl.BlockSpec((nt, d), lambda i, hd: (i, 0)),
            pl.BlockSpec((nt // LANES, rh * PEER_TOPK, LANES), lambda i, hd: (i, hd, 0)),
            pl.BlockSpec((nt // LANES, rh * PEER_TOPK, LANES), lambda i, hd: (i, hd, 0)),
        ],
        out_shape=[
            jax.ShapeDtypeStruct((n, d), BF16),
            jax.ShapeDtypeStruct((n // LANES, PEER_HEADS * PEER_TOPK, LANES), F32),
            jax.ShapeDtypeStruct((n // LANES, PEER_HEADS * PEER_TOPK, LANES), F32),
        ],
        scratch_shapes=[
            pltpu.VMEM((nt, d), BF16),
            pltpu.VMEM((nt, rh * dk2), F32),
        ] + [pltpu.VMEM((PEER_TOPK, LANES), F32)] * (6 * rh * (nt // LANES)),
        compiler_params=pltpu.CompilerParams(
            dimension_semantics=("arbitrary", "arbitrary"), vmem_limit_bytes=VMEM_LIMIT),
        name="route",
    )(x1, mod3, mod3, norm2, w_q, keys)


def _expert_kernel(h2_ref, e_ref, g_ref, u_ref, v_ref, x1_ref, g2_ref, fnorm_ref,
                   y_ref, s_s, erow_s, grow_s, w2_s, acc_s, *, ntok):
    c = pl.program_id(1)
    nchunks = pl.num_programs(1)
    nk = PEER_NKEYS

    @pl.when(c == 0)
    def _():
        acc_s[...] = jnp.zeros_like(acc_s)
        for sb in range(ntok // LANES):
            erow_s[sb * LANES:(sb + 1) * LANES, :] = e_ref[sb].T.astype(I32)
            grow_s[sb * LANES:(sb + 1) * LANES, :] = g_ref[sb].T
        riota = lax.broadcasted_iota(I32, (nk, nk), 0)
        riota2 = lax.broadcasted_iota(I32, (2 * nk, nk), 0)

        def pair(p):
            n0 = pl.multiple_of(2 * p, 2)
            lts, rts = [], []
            for t in range(2):
                e = erow_s[pl.ds(n0 + t, 1), :]
                gv = grow_s[pl.ds(n0 + t, 1), :]
                i1 = e >> 7
                i2 = e & (nk - 1)
                lts.append(jnp.where(riota2 == 2 * i1 + t, gv, 0.0).astype(BF16))
                rts.append(jnp.where(riota == i2, 1.0, 0.0).astype(BF16))
            x = _dot_nt(jnp.concatenate(lts, axis=1), jnp.concatenate(rts, axis=1))
            words = pltpu.bitcast(x.astype(BF16), I32)
            s_s[pl.ds(pl.multiple_of(p * S_PITCH, SUBLANES), nk), :] = words

        def pair_group(pg, carry):
            for q in range(PAIR_UNROLL):
                pair(pg * PAIR_UNROLL + q)
            return carry

        lax.fori_loop(0, ntok // (2 * PAIR_UNROLL), pair_group, 0)

    h2 = h2_ref[...]
    i1_per_sub = EXP_SUB // nk
    for j in range(EXP_CHUNK // EXP_SUB):
        a = _dot_nt(h2, u_ref[j * EXP_SUB:(j + 1) * EXP_SUB, :])
        i1_0 = c * (EXP_CHUNK // nk) + j * i1_per_sub
        gs = [pltpu.bitcast(s_s[pl.ds(i1_0 + t, ntok // 2, stride=S_PITCH), :], BF16)
              for t in range(i1_per_sub)]
        g = jnp.concatenate(gs, axis=1).astype(F32)
        w2_s[:, j * EXP_SUB:(j + 1) * EXP_SUB] = (_gelu(a) * g).astype(BF16)
    acc_s[...] += _dot(w2_s[...], v_ref[...])

    @pl.when(c == nchunks - 1)
    def _():
        x2 = x1_ref[...] + _mod_rows(g2_ref, ntok) * acc_s[...]
        y_ref[...] = _rmsnorm(x2, fnorm_ref[...])


def _expert(h2, eT, gT, u_bf, v_bf, x1, mod3, fnorm, *, seq_tokens):
    n, d = x1.shape
    nt = EXP_TOK
    ne = u_bf.shape[0]
    kern = functools.partial(_expert_kernel, ntok=nt)
    return pl.pallas_call(
        kern,
        grid=(n // nt, ne // EXP_CHUNK),
        in_specs=[
            pl.BlockSpec((nt, d), lambda i, c: (i, 0)),
            pl.BlockSpec((nt // LANES, PEER_HEADS * PEER_TOPK, LANES), lambda i, c: (i, 0, 0)),
            pl.BlockSpec((nt // LANES, PEER_HEADS * PEER_TOPK, LANES), lambda i, c: (i, 0, 0)),
            pl.BlockSpec((EXP_CHUNK, d), lambda i, c: (c, 0)),
            pl.BlockSpec((EXP_CHUNK, d), lambda i, c: (c, 0)),
            pl.BlockSpec((nt, d), lambda i, c: (i, 0)),
            _mod_spec(seq_tokens, nt, d, 5, lambda i, c: i),
            pl.BlockSpec((1, d), lambda i, c: (0, 0)),
        ],
        out_specs=pl.BlockSpec((nt, d), lambda i, c: (i, 0)),
        out_shape=jax.ShapeDtypeStruct((n, d), F32),
        scratch_shapes=[
            pltpu.VMEM((nt // 2 * S_PITCH, LANES), I32),
            pltpu.VMEM((nt, PEER_HEADS * PEER_TOPK), I32),
            pltpu.VMEM((nt, PEER_HEADS * PEER_TOPK), F32),
            pltpu.VMEM((nt, EXP_CHUNK), BF16),
            pltpu.VMEM((nt, d), F32),
        ],
        compiler_params=pltpu.CompilerParams(
            dimension_semantics=("arbitrary", "arbitrary"), vmem_limit_bytes=VMEM_LIMIT),
        name="expert",
    )(h2, eT, gT, u_bf, v_bf, x1, mod3, fnorm)


def _run_group(x, mod, pool0, conv0, h0, wts, *, bb, seg, pos0):
    bsz, t, d = x.shape
    n = bsz * t
    mod3 = mod[:, None, :]
    x1, pool_o, conv_o, h_o = _mixer(x, mod3, pool0, conv0, h0[:, None, :], wts, bb=bb, seg=seg, pos0=pos0)
    x1f = x1.reshape(n, d)
    h2, eT, gT = _route(x1f, mod3, wts["norm2"], wts["w_q"], wts["keys"], seq_tokens=t)
    y = _expert(h2, eT, gT, wts["u"], wts["v"], x1f, mod3, wts["fnorm"], seq_tokens=t)
    return y.reshape(bsz, t, d), pool_o, conv_o, h_o[:, 0]


def kernel(x_prompt, x_sample, c_prompt, c_sample, state_pool, state_conv, state_lru, norm1, norm2, w_ada, b_ada, w_in, b_gate, pool_mix, pool_scale, conv_w, conv_b, w_rg, b_rg, w_ig, b_ig, lru_lambda, w_branch_a, w_branch_b, w_out, w_q, sub_keys, expert_u, expert_v, final_norm):
    depth = w_in.shape[0]
    assert depth == 1, "single-layer trunk"
    bp, tp, d = x_prompt.shape
    bs, ts, _ = x_sample.shape
    l = 0
    mod = _ada(jnp.concatenate([c_prompt, c_sample], axis=0), w_ada[l], b_ada[l])
    row = lambda v: v.reshape(1, -1)
    wts = {
        "norm1": row(norm1[l]), "norm2": row(norm2[l]), "fnorm": row(final_norm),
        "w_in": w_in[l].astype(BF16), "b_gate": row(b_gate[l]),
        "pool_mix": pool_mix[l].astype(BF16), "pool_scale": row(pool_scale[l]),
        "conv_w": conv_w[l], "conv_b": row(conv_b[l]),
        "w_gi": jnp.concatenate([w_rg[l], w_ig[l]], axis=-1).astype(BF16),
        "b_rg": row(b_rg[l]), "b_ig": row(b_ig[l]), "lam": row(lru_lambda[l]),
        "w_a": w_branch_a[l].astype(BF16), "w_b": w_branch_b[l].astype(BF16),
        "w_out": w_out[l].astype(BF16), "w_q": w_q[l].astype(BF16),
        "keys": sub_keys[l].astype(BF16),
        "u": expert_u[l].astype(BF16), "v": expert_v[l].astype(BF16),
    }
    zp = jnp.zeros((bp, state_pool.shape[2], state_pool.shape[3]), F32)
    zc = jnp.zeros((bp, state_conv.shape[2], state_conv.shape[3]), F32)
    zh = jnp.zeros((bp, state_lru.shape[2]), F32)
    y_p, pool_p, conv_p, lru_p = _run_group(
        x_prompt, mod[:bp], zp, zc, zh, wts, bb=1, seg=256, pos0=0)
    y_s, pool_s, conv_s, lru_s = _run_group(
        x_sample, mod[bp:], state_pool[l], state_conv[l], state_lru[l], wts,
        bb=32, seg=ts, pos0=PAST_LEN)
    st = lambda v: v[None]
    return (y_p, y_s, st(pool_p), st(conv_p), st(lru_p).astype(state_lru.dtype),
            st(pool_s), st(conv_s), st(lru_s).astype(state_lru.dtype))
```

```python
import functools

import jax
import jax.numpy as jnp
from jax import lax
from jax.experimental import pallas as pl
from jax.experimental.pallas import tpu as pltpu

F32 = jnp.float32
BF16 = jnp.bfloat16
I32 = jnp.int32

NORM_EPS = 1e-6
PAST_LEN = 16384
LRU_C = 8.0
POOL_WINDOWS = (2, 4, 8, 16)
POOL_ROWS = max(POOL_WINDOWS) - 1
POOL_HIST = 16
CONV_WIDTH = 4
CONV_ROWS = CONV_WIDTH - 1
CONV_HIST = 8
LRU_HEADS = 8
PEER_HEADS = 8
PEER_NKEYS = 128
PEER_TOPK = 16
SUBLANES = 8
LANES = 128
VMEM_LIMIT = 56 * 1024 * 1024

EXP_TOK = 512
EXP_CHUNK = 2048
EXP_SUB = 256
PAIR_UNROLL = 64
S_PITCH = 136
ROUTE_TOK = 1024
ROUTE_HEADS = 2


def _dot(a, b):
    return jnp.dot(a, b, preferred_element_type=F32)


def _dot_nt(a, b):
    return lax.dot_general(a, b, (((1,), (1,)), ((), ())), preferred_element_type=F32)


def _rmsnorm(x, g):
    return x * lax.rsqrt(jnp.mean(x * x, axis=-1, keepdims=True) + NORM_EPS) * g


def _gelu(x):
    return jax.nn.gelu(x, approximate=True)


def _mod_rows(m_ref, rows):
    m = m_ref[...]
    nseq, _, d = m.shape
    if nseq == 1:
        return m[0]
    return jnp.broadcast_to(m, (nseq, rows // nseq, d)).reshape(rows, d)


def _mod_spec(seq_tokens, tile_tokens, d, col, tile_of):
    if seq_tokens >= tile_tokens:
        tiles_per_seq = seq_tokens // tile_tokens
        return pl.BlockSpec((1, 1, d), lambda *g: (tile_of(*g) // tiles_per_seq, 0, col))
    return pl.BlockSpec((tile_tokens // seq_tokens, 1, d), lambda *g: (tile_of(*g), 0, col))


def _ada_kernel(c_ref, w_ref, b_ref, o_ref):
    c = c_ref[...]
    s = (c * jax.nn.sigmoid(c)).astype(BF16)
    o_ref[...] = _dot(s, w_ref[...].astype(BF16)) + b_ref[...]


def _ada(c, w_ada, b_ada):
    n, d = c.shape
    dout = w_ada.shape[1]
    tn = 1024
    return pl.pallas_call(
        _ada_kernel,
        grid=(dout // tn,),
        in_specs=[
            pl.BlockSpec((n, d), lambda j: (0, 0)),
            pl.BlockSpec((d, tn), lambda j: (0, j)),
            pl.BlockSpec((1, tn), lambda j: (0, j)),
        ],
        out_specs=pl.BlockSpec((n, tn), lambda j: (0, j)),
        out_shape=jax.ShapeDtypeStruct((n, dout), F32),
        compiler_params=pltpu.CompilerParams(
            dimension_semantics=("arbitrary",), vmem_limit_bytes=VMEM_LIMIT),
        name="ada",
    )(c, w_ada, b_ada.reshape(1, dout))


def _mixer_kernel(x_ref, sh_ref, sc_ref, g1_ref, pool0_ref, conv0_ref, h0_ref,
                  norm1_ref, w_in_ref, b_gate_ref, pool_mix_ref, pool_scale_ref,
                  conv_w_ref, conv_b_ref, w_gi_ref, b_rg_ref, b_ig_ref, lam_ref,
                  w_a_ref, w_b_ref, w_out_ref,
                  x1_ref, pool_out_ref, conv_out_ref, h_out_ref,
                  ext_u, ext_c, gate_s, a_s, b_s, hs_s, hcar, proj_s,
                  *, bb, seg, pos0, width):
    rows = bb * seg
    tt = pl.program_id(1)
    w = width
    gdim = w // len(POOL_WINDOWS)
    hdim = w // LRU_HEADS

    @pl.when(tt == 0)
    def _():
        ext_u[:, 0:POOL_HIST, :] = jnp.zeros((bb, POOL_HIST, w), F32)
        ext_u[:, POOL_HIST - POOL_ROWS:POOL_HIST, :] = pool0_ref[...]
        ext_c[:, 0:CONV_HIST, :] = jnp.zeros((bb, CONV_HIST, w), F32)
        ext_c[:, CONV_HIST - CONV_ROWS:CONV_HIST, :] = conv0_ref[...]
        hcar[...] = jnp.broadcast_to(h0_ref[...], (bb, SUBLANES, w)).reshape(bb * SUBLANES, w)

    x = x_ref[...].reshape(rows, w)
    h = _rmsnorm(x, norm1_ref[...]) * (1.0 + _mod_rows(sc_ref, rows)) + _mod_rows(sh_ref, rows)
    hb = h.astype(BF16)

    row = lax.broadcasted_iota(I32, (rows, 1), 0)
    pos = pos0 + tt * seg + (row & (seg - 1))

    proj_s[...] = _dot(hb, w_in_ref[...])
    u = proj_s[:, 0:w]
    ext_u[:, POOL_HIST:POOL_HIST + seg, :] = u.reshape(bb, seg, w)
    ext_rows = bb * (POOL_HIST + seg)
    wsum = ext_u[:, :, 0:w].reshape(ext_rows, w)
    level = 1
    for g, win in enumerate(POOL_WINDOWS):
        lo = g * gdim
        while level < win:
            wsum = wsum + pltpu.roll(wsum, level, 0)
            level *= 2
        acc = wsum[:, 0:gdim].reshape(bb, POOL_HIST + seg, gdim)[:, POOL_HIST:, :].reshape(rows, gdim)
        wsum = wsum[:, gdim:]
        cnt = jnp.minimum(pos + 1, win).astype(F32)
        pooled = acc / cnt - u[:, lo:lo + gdim]
        zg = _dot(pooled.astype(BF16), pool_mix_ref[g])
        gate_s[:, lo:lo + gdim] = zg
    z = gate_s[...] * pool_scale_ref[...]
    y_a = _dot(z.astype(BF16), w_a_ref[...])
    pool_out_ref[...] = ext_u[:, seg + POOL_HIST - POOL_ROWS:seg + POOL_HIST, :]
    ext_u[:, 0:POOL_HIST, :] = ext_u[:, seg:seg + POOL_HIST, :]

    g_a = jax.nn.sigmoid(proj_s[:, 3 * w:4 * w] + b_gate_ref[:, 0:w])
    m = g_a * y_a

    xb = proj_s[:, w:2 * w]
    ext_c[:, CONV_HIST:CONV_HIST + seg, :] = xb.reshape(bb, seg, w)
    xc = conv_b_ref[...] + xb * conv_w_ref[CONV_WIDTH - 1:CONV_WIDTH, :]
    for k in range(CONV_WIDTH - 1):
        sh = CONV_WIDTH - 1 - k
        xc = xc + ext_c[:, CONV_HIST - sh:CONV_HIST - sh + seg, :].reshape(rows, w) * conv_w_ref[k:k + 1, :]
    conv_out_ref[...] = ext_c[:, seg + CONV_HIST - CONV_ROWS:seg + CONV_HIST, :]
    ext_c[:, 0:CONV_HIST, :] = ext_c[:, seg:seg + CONV_HIST, :]

    xcb = xc.astype(BF16)
    lam = -lam_ref[...]
    softplus = jnp.maximum(lam, 0.0) + jnp.log1p(jnp.exp(-jnp.abs(lam)))
    for hh in range(LRU_HEADS):
        lo = hh * hdim
        ri = _dot(xcb[:, lo:lo + hdim], w_gi_ref[hh])
        r = jax.nn.sigmoid(ri[:, 0:hdim] + b_rg_ref[:, lo:lo + hdim])
        i = jax.nn.sigmoid(ri[:, hdim:2 * hdim] + b_ig_ref[:, lo:lo + hdim])
        log_a = (-LRU_C) * r * softplus[:, lo:lo + hdim]
        a = jnp.exp(log_a)
        mult = jnp.sqrt(-jnp.tanh(log_a) * (a * a + 1.0))
        mult = jnp.where(pos == 0, 1.0, mult)
        a_s[:, lo:lo + hdim] = a
        b_s[:, lo:lo + hdim] = mult * i * xc[:, lo:lo + hdim]

    tiles = rows // SUBLANES
    a = a_s[...].reshape(tiles, SUBLANES, w)
    b = b_s[...].reshape(tiles, SUBLANES, w)
    rin = lax.broadcasted_iota(I32, (1, SUBLANES, 1), 1)
    for d in (1, 2, 4):
        a_sh = jnp.where(rin >= d, pltpu.roll(a, d, 1), 1.0)
        b_sh = jnp.where(rin >= d, pltpu.roll(b, d, 1), 0.0)
        b = a * b_sh + b
        a = a * a_sh
    a = a.reshape(rows, w)
    b = b.reshape(rows, w)
    if seg == SUBLANES:
        hs = a * hcar[...] + b
        h_out_ref[...] = hs.reshape(bb, SUBLANES, w)[:, SUBLANES - 1:SUBLANES, :]
    else:
        a_s[...] = a
        b_s[...] = b

        def tile_step(k, hp):
            r0 = pl.multiple_of(k * SUBLANES, SUBLANES)
            ht = a_s[pl.ds(r0, SUBLANES), :] * hp + b_s[pl.ds(r0, SUBLANES), :]
            hs_s[pl.ds(r0, SUBLANES), :] = ht
            return jnp.broadcast_to(ht[SUBLANES - 1:SUBLANES, :], (SUBLANES, w))

        hp = lax.fori_loop(0, rows // SUBLANES, tile_step, hcar[...])
        hcar[...] = hp
        h_out_ref[...] = hp.reshape(bb, SUBLANES, w)[:, SUBLANES - 1:SUBLANES, :]
        hs = hs_s[...]

    yb = proj_s[:, 2 * w:3 * w]
    y_b = _dot((hs * _gelu(yb)).astype(BF16), w_b_ref[...])
    g_b = jax.nn.sigmoid(proj_s[:, 4 * w:5 * w] + b_gate_ref[:, w:2 * w])
    m = m + g_b * y_b
    out = _dot(m.astype(BF16), w_out_ref[...])
    x1_ref[...] = (x + _mod_rows(g1_ref, rows) * out).reshape(bb, seg, w)


def _const_spec(shape):
    nd = len(shape)
    return pl.BlockSpec(shape, lambda *_: (0,) * nd, pipeline_mode=pl.Buffered(1))


_MIXER_WEIGHTS = ("norm1", "w_in", "b_gate", "pool_mix", "pool_scale", "conv_w", "conv_b",
                  "w_gi", "b_rg", "b_ig", "lam", "w_a", "w_b", "w_out")
_MIXER_INPUTS = 7 + len(_MIXER_WEIGHTS)
_MIXER_OUTPUTS = 4


def _mixer_cast_kernel(*refs, ntab, **kw):
    ins = refs[:_MIXER_INPUTS + ntab]
    outs = refs[len(ins):len(ins) + _MIXER_OUTPUTS + ntab]
    scratch = refs[len(ins) + len(outs):]
    for k in range(ntab):
        outs[_MIXER_OUTPUTS + k][...] = ins[_MIXER_INPUTS + k][...].astype(BF16)
    _mixer_kernel(*ins[:_MIXER_INPUTS], *outs[:_MIXER_OUTPUTS], *scratch, **kw)


def _mixer(x, mod3, pool0, conv0, h0e, weights, *, bb, seg, pos0, tables=()):
    bsz, t, d = x.shape
    w = d
    rows = bb * seg
    nb, nt = bsz // bb, t // seg
    assert bb == 1 or nt == 1, "a tile is part of one sequence or holds whole sequences"
    steps = nb * nt
    tab_specs = [pl.BlockSpec((tb.shape[0] // steps, tb.shape[1]), lambda b, tt: (b * nt + tt, 0))
                 for tb in tables]
    assert all(tb.shape[0] % (steps * 2 * SUBLANES) == 0 for tb in tables)
    mod_specs = [pl.BlockSpec((bb, 1, d), functools.partial(lambda b, tt, col: (b, 0, col), col=col))
                 for col in range(3)]
    wvals = [weights[k] for k in _MIXER_WEIGHTS]
    kern = functools.partial(_mixer_cast_kernel, ntab=len(tables), bb=bb, seg=seg, pos0=pos0, width=w)
    return pl.pallas_call(
        kern,
        grid=(nb, nt),
        in_specs=[
            pl.BlockSpec((bb, seg, d), lambda b, tt: (b, tt, 0)),
            *mod_specs,
            pl.BlockSpec((bb, POOL_ROWS, w), lambda b, tt: (b, 0, 0)),
            pl.BlockSpec((bb, CONV_ROWS, w), lambda b, tt: (b, 0, 0)),
            pl.BlockSpec((bb, 1, w), lambda b, tt: (b, 0, 0)),
        ] + [_const_spec(v.shape) for v in wvals] + tab_specs,
        out_specs=[
            pl.BlockSpec((bb, seg, d), lambda b, tt: (b, tt, 0)),
            pl.BlockSpec((bb, POOL_ROWS, w), lambda b, tt: (b, 0, 0)),
            pl.BlockSpec((bb, CONV_ROWS, w), lambda b, tt: (b, 0, 0)),
            pl.BlockSpec((bb, 1, w), lambda b, tt: (b, 0, 0)),
        ] + tab_specs,
        out_shape=[
            jax.ShapeDtypeStruct((bsz, t, d), F32),
            jax.ShapeDtypeStruct((bsz, POOL_ROWS, w), F32),
            jax.ShapeDtypeStruct((bsz, CONV_ROWS, w), F32),
            jax.ShapeDtypeStruct((bsz, 1, w), F32),
        ] + [jax.ShapeDtypeStruct(tb.shape, BF16) for tb in tables],
        scratch_shapes=[
            pltpu.VMEM((bb, POOL_HIST + seg, w), F32),
            pltpu.VMEM((bb, CONV_HIST + seg, w), F32),
            pltpu.VMEM((rows, w), F32),
            pltpu.VMEM((rows, w), F32),
            pltpu.VMEM((rows, w), F32),
            pltpu.VMEM((rows, w), F32),
            pltpu.VMEM((bb * SUBLANES, w), F32),
            pltpu.VMEM((rows, weights["w_in"].shape[1]), F32),
        ],
        compiler_params=pltpu.CompilerParams(
            dimension_semantics=("arbitrary", "arbitrary"), vmem_limit_bytes=VMEM_LIMIT),
        name="mixer",
    )(x, mod3, mod3, mod3, pool0, conv0, h0e, *wvals, *tables)


def _sort16_network():
    n, pairs, p = 16, [], 1
    while p < n:
        k = p
        while k >= 1:
            for j in range(k % p, n - k, 2 * k):
                for i in range(min(k, n - j - k)):
                    if (i + j) // (2 * p) == (i + j + k) // (2 * p):
                        pairs.append((i + j, i + j + k))
            k //= 2
        p *= 2
    return pairs


def _top16_rows(s, val_ref, idx_ref):
    nt = s.shape[0] // SUBLANES
    assert nt == PEER_TOPK
    sub = lax.broadcasted_iota(I32, (SUBLANES, LANES), 0).astype(F32)
    v = [s[SUBLANES * k:SUBLANES * (k + 1), :] for k in range(nt)]
    ix = [sub + float(SUBLANES * k) for k in range(nt)]
    for i, j in _sort16_network():
        swap = (v[j] > v[i]) | ((v[j] == v[i]) & (ix[j] < ix[i]))
        v[i], v[j] = jnp.where(swap, v[j], v[i]), jnp.where(swap, v[i], v[j])
        ix[i], ix[j] = jnp.where(swap, ix[j], ix[i]), jnp.where(swap, ix[i], ix[j])
    for it in range(PEER_TOPK):
        mx = jnp.max(v[0], axis=0, keepdims=True)
        best = jnp.min(jnp.where(v[0] == mx, ix[0], float(s.shape[0])), axis=0, keepdims=True)
        val_ref[it:it + 1, :] = mx
        idx_ref[it:it + 1, :] = best
        win = ix[0] == best
        for r in range(PEER_TOPK - 1 - it):
            v[r] = jnp.where(win, v[r + 1], v[r])
            ix[r] = jnp.where(win, ix[r + 1], ix[r])


def _route_kernel(x_ref, sh_ref, sc_ref, norm2_ref, wq_ref, keys_ref,
                  h2_ref, e_ref, g_ref, h2_s, q_s, *list_refs, ntok):
    hd = pl.program_id(1)
    nsub = ntok // LANES
    lists = [list_refs[6 * u:6 * u + 6] for u in range(ROUTE_HEADS * nsub)]

    @pl.when(hd == 0)
    def _():
        x = x_ref[...]
        h2 = _rmsnorm(x, norm2_ref[...]) * (1.0 + _mod_rows(sc_ref, ntok)) + _mod_rows(sh_ref, ntok)
        h2b = h2.astype(BF16)
        h2_s[...] = h2b
        h2_ref[...] = h2b

    q_s[...] = _dot(h2_s[...], wq_ref[...])
    half = PEER_NKEYS

    sub = lax.broadcasted_iota(I32, (SUBLANES, LANES), 0)
    is_row = sub < 4
    nkf = float(PEER_NKEYS)
    nexp = nkf * nkf
    depth = jnp.full((SUBLANES, LANES), 0, I32)
    for r, dep in enumerate((16, 8, 5, 4, 12, 4, 1)):
        depth = jnp.where(sub == r, dep, depth)
    subf = sub.astype(F32)
    flat0 = jnp.where(is_row, subf * PEER_TOPK, 4.0 * PEER_TOPK + (subf - 4.0)) * nexp
    dflat = jnp.where(is_row, 1.0, float(PEER_TOPK)) * nexp

    def sub_block(hh, j):
        v1_s, i1_s, v2_s, i2_s, bs_s, bk_s = lists[hh * nsub + j]
        qj = q_s[j * LANES:(j + 1) * LANES, 2 * half * hh:2 * half * (hh + 1)].astype(BF16)
        _top16_rows(_dot_nt(keys_ref[hh, 0], qj[:, 0:half]), v1_s, i1_s)
        _top16_rows(_dot_nt(keys_ref[hh, 1], qj[:, half:2 * half]), v2_s, i2_s)
        vfix = jnp.where(is_row, v1_s[0:SUBLANES, :], pltpu.roll(v2_s[0:SUBLANES, :], 4, 0))
        efix = jnp.where(is_row, i1_s[0:SUBLANES, :] * nkf, pltpu.roll(i2_s[0:SUBLANES, :], 4, 0))
        t, k = [], []
        for d in range(PEER_TOPK):
            if d + 4 < PEER_TOPK:
                vrun = jnp.where(is_row, v2_s[d:d + 1, :], v1_s[d + 4:d + 5, :])
                erun = jnp.where(is_row, i2_s[d:d + 1, :], i1_s[d + 4:d + 5, :] * nkf)
            else:
                vrun = v2_s[d:d + 1, :]
                erun = i2_s[d:d + 1, :]
            t.append(jnp.where(depth > d, vfix + vrun, -jnp.inf))
            k.append(flat0 + dflat * float(d) + (efix + erun))
        for it in range(PEER_TOPK):
            mx = jnp.max(t[0], axis=0, keepdims=True)
            kx = jnp.min(jnp.where(t[0] == mx, k[0], PEER_TOPK * PEER_TOPK * nexp), axis=0, keepdims=True)
            bk_s[it:it + 1, :] = kx
            bs_s[it:it + 1, :] = mx
            win = k[0] == kx
            for r in range(PEER_TOPK - 1 - it):
                t[r] = jnp.where(win, t[r + 1], t[r])
                k[r] = jnp.where(win, k[r + 1], k[r])
        best = bs_s[...]
        ex = jnp.exp(best - best[0:1, :])
        gsm = ex / jnp.sum(ex, axis=0, keepdims=True)
        bk = bk_s[...]
        rows = slice(hh * PEER_TOPK, (hh + 1) * PEER_TOPK)
        e_ref[j, rows, :] = bk - jnp.floor(bk / nexp) * nexp
        g_ref[j, rows, :] = gsm

    for hh in range(ROUTE_HEADS):
        for j in range(nsub):
            sub_block(hh, j)


def _route(x1, mod3, norm2, w_q, keys, *, seq_tokens):
    n, d = x1.shape
    nt = ROUTE_TOK
    tiles = n // nt
    mod_specs = [_mod_spec(seq_tokens, nt, d, col, lambda i, hd: i) for col in (3, 4)]
    dk2 = w_q.shape[1] // PEER_HEADS
    rh = ROUTE_HEADS
    kern = functools.partial(_route_kernel, ntok=nt)
    return pl.pallas_call(
        kern,
        grid=(tiles, PEER_HEADS // rh),
        in_specs=[
            pl.BlockSpec((nt, d), lambda i, hd: (i, 0)),
            *mod_specs,
            pl.BlockSpec((1, d), lambda i, hd: (0, 0)),
            pl.BlockSpec((d, rh * dk2), lambda i, hd: (0, hd)),
            pl.BlockSpec((rh, 2, PEER_NKEYS, dk2 // 2), lambda i, hd: (hd, 0, 0, 0)),
        ],
        out_specs=[
            pl.BlockSpec((nt, d), lambda i, hd: (i, 0)),
            pl.BlockSpec((nt // LANES, rh * PEER_TOPK, LANES), lambda i, hd: (i, hd, 0)),
            pl.BlockSpec((nt // LANES, rh * PEER_TOPK, LANES), lambda i, hd: (i, hd, 0)),
        ],
        out_shape=[
            jax.ShapeDtypeStruct((n, d), BF16),
            jax.ShapeDtypeStruct((n // LANES, PEER_HEADS * PEER_TOPK, LANES), F32),
            jax.ShapeDtypeStruct((n // LANES, PEER_HEADS * PEER_TOPK, LANES), F32),
        ],
        scratch_shapes=[
            pltpu.VMEM((nt, d), BF16),
            pltpu.VMEM((nt, rh * dk2), F32),
        ] + [pltpu.VMEM((PEER_TOPK, LANES), F32)] * (6 * rh * (nt // LANES)),
        compiler_params=pltpu.CompilerParams(
            dimension_semantics=("arbitrary", "arbitrary"), vmem_limit_bytes=VMEM_LIMIT),
        name="route",
    )(x1, mod3, mod3, norm2, w_q, keys)


def _expert_kernel(h2_ref, e_ref, g_ref, u_ref, v_ref, x1_ref, g2_ref, fnorm_ref,
                   y_ref, s_s, erow_s, grow_s, w2_s, acc_s, *, ntok):
    c = pl.program_id(1)
    nchunks = pl.num_programs(1)
    nk = PEER_NKEYS

    @pl.when(c == 0)
    def _():
        acc_s[...] = jnp.zeros_like(acc_s)
        for sb in range(ntok // LANES):
            erow_s[sb * LANES:(sb + 1) * LANES, :] = e_ref[sb].T.astype(I32)
            grow_s[sb * LANES:(sb + 1) * LANES, :] = g_ref[sb].T
        riota = lax.broadcasted_iota(I32, (nk, nk), 0)
        riota2 = lax.broadcasted_iota(I32, (2 * nk, nk), 0)

        def pair(p):
            n0 = pl.multiple_of(2 * p, 2)
            lts, rts = [], []
            for t in range(2):
                e = erow_s[pl.ds(n0 + t, 1), :]
                gv = grow_s[pl.ds(n0 + t, 1), :]
                i1 = e >> 7
                i2 = e & (nk - 1)
                lts.append(jnp.where(riota2 == 2 * i1 + t, gv, 0.0).astype(BF16))
                rts.append(jnp.where(riota == i2, 1.0, 0.0).astype(BF16))
            x = _dot_nt(jnp.concatenate(lts, axis=1), jnp.concatenate(rts, axis=1))
            words = pltpu.bitcast(x.astype(BF16), I32)
            s_s[pl.ds(pl.multiple_of(p * S_PITCH, SUBLANES), nk), :] = words

        def pair_group(pg, carry):
            for q in range(PAIR_UNROLL):
                pair(pg * PAIR_UNROLL + q)
            return carry

        lax.fori_loop(0, ntok // (2 * PAIR_UNROLL), pair_group, 0)

    h2 = h2_ref[...]
    i1_per_sub = EXP_SUB // nk
    for j in range(EXP_CHUNK // EXP_SUB):
        a = _dot_nt(h2, u_ref[j * EXP_SUB:(j + 1) * EXP_SUB, :])
        i1_0 = c * (EXP_CHUNK // nk) + j * i1_per_sub
        gs = [pltpu.bitcast(s_s[pl.ds(i1_0 + t, ntok // 2, stride=S_PITCH), :], BF16)
              for t in range(i1_per_sub)]
        g = jnp.concatenate(gs, axis=1).astype(F32)
        w2_s[:, j * EXP_SUB:(j + 1) * EXP_SUB] = (_gelu(a) * g).astype(BF16)
    acc_s[...] += _dot(w2_s[...], v_ref[...])

    @pl.when(c == nchunks - 1)
    def _():
        x2 = x1_ref[...] + _mod_rows(g2_ref, ntok) * acc_s[...]
        y_ref[...] = _rmsnorm(x2, fnorm_ref[...])


def _expert(h2, eT, gT, u_bf, v_bf, x1, mod3, fnorm, *, seq_tokens):
    n, d = x1.shape
    nt = EXP_TOK
    ne = u_bf.shape[0]
    kern = functools.partial(_expert_kernel, ntok=nt)
    return pl.pallas_call(
        kern,
        grid=(n // nt, ne // EXP_CHUNK),
        in_specs=[
            pl.BlockSpec((nt, d), lambda i, c: (i, 0)),
            pl.BlockSpec((nt // LANES, PEER_HEADS * PEER_TOPK, LANES), lambda i, c: (i, 0, 0)),
            pl.BlockSpec((nt // LANES, PEER_HEADS * PEER_TOPK, LANES), lambda i, c: (i, 0, 0)),
            pl.BlockSpec((EXP_CHUNK, d), lambda i, c: (c, 0)),
            pl.BlockSpec((EXP_CHUNK, d), lambda i, c: (c, 0)),
            pl.BlockSpec((nt, d), lambda i, c: (i, 0)),
            _mod_spec(seq_tokens, nt, d, 5, lambda i, c: i),
            pl.BlockSpec((1, d), lambda i, c: (0, 0)),
        ],
        out_specs=pl.BlockSpec((nt, d), lambda i, c: (i, 0)),
        out_shape=jax.ShapeDtypeStruct((n, d), F32),
        scratch_shapes=[
            pltpu.VMEM((nt // 2 * S_PITCH, LANES), I32),
            pltpu.VMEM((nt, PEER_HEADS * PEER_TOPK), I32),
            pltpu.VMEM((nt, PEER_HEADS * PEER_TOPK), F32),
            pltpu.VMEM((nt, EXP_CHUNK), BF16),
            pltpu.VMEM((nt, d), F32),
        ],
        compiler_params=pltpu.CompilerParams(
            dimension_semantics=("arbitrary", "arbitrary"), vmem_limit_bytes=VMEM_LIMIT),
        name="expert",
    )(h2, eT, gT, u_bf, v_bf, x1, mod3, fnorm)


def _run_group(x, mod, pool0, conv0, h0, wts, expert_tables, *, bb, seg, pos0):
    bsz, t, d = x.shape
    n = bsz * t
    mod3 = mod[:, None, :]
    cast = tuple(tb for tb in expert_tables if tb.dtype != BF16)
    x1, pool_o, conv_o, h_o, *cast_out = _mixer(
        x, mod3, pool0, conv0, h0[:, None, :], wts, bb=bb, seg=seg, pos0=pos0, tables=cast)
    u_bf, v_bf = cast_out if cast else expert_tables
    x1f = x1.reshape(n, d)
    h2, eT, gT = _route(x1f, mod3, wts["norm2"], wts["w_q"], wts["keys"], seq_tokens=t)
    y = _expert(h2, eT, gT, u_bf, v_bf, x1f, mod3, wts["fnorm"], seq_tokens=t)
    return (y.reshape(bsz, t, d), pool_o, conv_o, h_o[:, 0]), (u_bf, v_bf)


def kernel(x_prompt, x_sample, c_prompt, c_sample, state_pool, state_conv, state_lru, norm1, norm2, w_ada, b_ada, w_in, b_gate, pool_mix, pool_scale, conv_w, conv_b, w_rg, b_rg, w_ig, b_ig, lru_lambda, w_branch_a, w_branch_b, w_out, w_q, sub_keys, expert_u, expert_v, final_norm):
    depth = w_in.shape[0]
    assert depth == 1, "single-layer trunk"
    bp, tp, d = x_prompt.shape
    bs, ts, _ = x_sample.shape
    l = 0
    mod = _ada(jnp.concatenate([c_prompt, c_sample], axis=0), w_ada[l], b_ada[l])
    row = lambda v: v.reshape(1, -1)
    wts = {
        "norm1": row(norm1[l]), "norm2": row(norm2[l]), "fnorm": row(final_norm),
        "w_in": w_in[l].astype(BF16), "b_gate": row(b_gate[l]),
        "pool_mix": pool_mix[l].astype(BF16), "pool_scale": row(pool_scale[l]),
        "conv_w": conv_w[l], "conv_b": row(conv_b[l]),
        "w_gi": jnp.concatenate([w_rg[l], w_ig[l]], axis=-1).astype(BF16),
        "b_rg": row(b_rg[l]), "b_ig": row(b_ig[l]), "lam": row(lru_lambda[l]),
        "w_a": w_branch_a[l].astype(BF16), "w_b": w_branch_b[l].astype(BF16),
        "w_out": w_out[l].astype(BF16), "w_q": w_q[l].astype(BF16),
        "keys": sub_keys[l].astype(BF16),
    }
    zp = jnp.zeros((bp, state_pool.shape[2], state_pool.shape[3]), F32)
    zc = jnp.zeros((bp, state_conv.shape[2], state_conv.shape[3]), F32)
    zh = jnp.zeros((bp, state_lru.shape[2]), F32)
    (y_p, pool_p, conv_p, lru_p), tables_bf = _run_group(
        x_prompt, mod[:bp], zp, zc, zh, wts, (expert_u[l], expert_v[l]), bb=1, seg=256, pos0=0)
    (y_s, pool_s, conv_s, lru_s), _ = _run_group(
        x_sample, mod[bp:], state_pool[l], state_conv[l], state_lru[l], wts, tables_bf,
        bb=32, seg=ts, pos0=PAST_LEN)
    st = lambda v: v[None]
    return (y_p, y_s, st(pool_p), st(conv_p), st(lru_p).astype(state_lru.dtype),
            st(pool_s), st(conv_s), st(lru_s).astype(state_lru.dtype))
```

```python
import functools

import jax
import jax.numpy as jnp
from jax import lax
from jax.experimental import pallas as pl
from jax.experimental.pallas import tpu as pltpu

F32 = jnp.float32
BF16 = jnp.bfloat16
I32 = jnp.int32

NORM_EPS = 1e-6
PAST_LEN = 16384
LRU_C = 8.0
POOL_WINDOWS = (2, 4, 8, 16)
POOL_ROWS = max(POOL_WINDOWS) - 1
POOL_HIST = 16
CONV_WIDTH = 4
CONV_ROWS = CONV_WIDTH - 1
CONV_HIST = 8
LRU_HEADS = 8
PEER_HEADS = 8
PEER_NKEYS = 128
PEER_TOPK = 16
SUBLANES = 8
LANES = 128
VMEM_LIMIT = 56 * 1024 * 1024

EXP_TOK = 512
EXP_CHUNK = 2048
EXP_SUB = 256
PAIR_UNROLL = 64
S_PITCH = 136
ROUTE_TOK = 1024
ROUTE_HEADS = 4


def _dot(a, b):
    return jnp.dot(a, b, preferred_element_type=F32)


def _dot_nt(a, b):
    return lax.dot_general(a, b, (((1,), (1,)), ((), ())), preferred_element_type=F32)


def _rmsnorm(x, g):
    return x * lax.rsqrt(jnp.mean(x * x, axis=-1, keepdims=True) + NORM_EPS) * g


def _gelu(x):
    return jax.nn.gelu(x, approximate=True)


def _mod_rows(m_ref, rows):
    m = m_ref[...]
    nseq, _, d = m.shape
    if nseq == 1:
        return m[0]
    return jnp.broadcast_to(m, (nseq, rows // nseq, d)).reshape(rows, d)


def _mod_spec(seq_tokens, tile_tokens, d, col, tile_of):
    if seq_tokens >= tile_tokens:
        tiles_per_seq = seq_tokens // tile_tokens
        return pl.BlockSpec((1, 1, d), lambda *g: (tile_of(*g) // tiles_per_seq, 0, col))
    return pl.BlockSpec((tile_tokens // seq_tokens, 1, d), lambda *g: (tile_of(*g), 0, col))


def _ada_kernel(c_ref, w_ref, b_ref, o_ref):
    c = c_ref[...]
    s = (c * jax.nn.sigmoid(c)).astype(BF16)
    o_ref[...] = _dot(s, w_ref[...].astype(BF16)) + b_ref[...]


def _ada(c, w_ada, b_ada):
    n, d = c.shape
    dout = w_ada.shape[1]
    tn = 1024
    return pl.pallas_call(
        _ada_kernel,
        grid=(dout // tn,),
        in_specs=[
            pl.BlockSpec((n, d), lambda j: (0, 0)),
            pl.BlockSpec((d, tn), lambda j: (0, j)),
            pl.BlockSpec((1, tn), lambda j: (0, j)),
        ],
        out_specs=pl.BlockSpec((n, tn), lambda j: (0, j)),
        out_shape=jax.ShapeDtypeStruct((n, dout), F32),
        compiler_params=pltpu.CompilerParams(
            dimension_semantics=("arbitrary",), vmem_limit_bytes=VMEM_LIMIT),
        name="ada",
    )(c, w_ada, b_ada.reshape(1, dout))


def _mixer_kernel(x_ref, sh_ref, sc_ref, g1_ref, pool0_ref, conv0_ref, h0_ref,
                  norm1_ref, w_in_ref, b_gate_ref, pool_mix_ref, pool_scale_ref,
                  conv_w_ref, conv_b_ref, w_gi_ref, b_rg_ref, b_ig_ref, lam_ref,
                  w_a_ref, w_b_ref, w_out_ref,
                  x1_ref, pool_out_ref, conv_out_ref, h_out_ref,
                  ext_u, ext_c, gate_s, a_s, b_s, hs_s, hcar, proj_s,
                  *, bb, seg, pos0, width):
    rows = bb * seg
    tt = pl.program_id(1)
    w = width
    gdim = w // len(POOL_WINDOWS)
    hdim = w // LRU_HEADS

    @pl.when(tt == 0)
    def _():
        ext_u[:, 0:POOL_HIST, :] = jnp.zeros((bb, POOL_HIST, w), F32)
        ext_u[:, POOL_HIST - POOL_ROWS:POOL_HIST, :] = pool0_ref[...]
        ext_c[:, 0:CONV_HIST, :] = jnp.zeros((bb, CONV_HIST, w), F32)
        ext_c[:, CONV_HIST - CONV_ROWS:CONV_HIST, :] = conv0_ref[...]
        hcar[...] = jnp.broadcast_to(h0_ref[...], (bb, SUBLANES, w)).reshape(bb * SUBLANES, w)

    x = x_ref[...].reshape(rows, w)
    h = _rmsnorm(x, norm1_ref[...]) * (1.0 + _mod_rows(sc_ref, rows)) + _mod_rows(sh_ref, rows)
    hb = h.astype(BF16)

    row = lax.broadcasted_iota(I32, (rows, 1), 0)
    pos = pos0 + tt * seg + (row & (seg - 1))

    proj_s[...] = _dot(hb, w_in_ref[...])
    u = proj_s[:, 0:w]
    ext_u[:, POOL_HIST:POOL_HIST + seg, :] = u.reshape(bb, seg, w)
    ext_rows = bb * (POOL_HIST + seg)
    wsum = ext_u[:, :, 0:w].reshape(ext_rows, w)
    level = 1
    for g, win in enumerate(POOL_WINDOWS):
        lo = g * gdim
        while level < win:
            wsum = wsum + pltpu.roll(wsum, level, 0)
            level *= 2
        acc = wsum[:, 0:gdim].reshape(bb, POOL_HIST + seg, gdim)[:, POOL_HIST:, :].reshape(rows, gdim)
        wsum = wsum[:, gdim:]
        cnt = jnp.minimum(pos + 1, win).astype(F32)
        pooled = acc / cnt - u[:, lo:lo + gdim]
        zg = _dot(pooled.astype(BF16), pool_mix_ref[g])
        gate_s[:, lo:lo + gdim] = zg
    z = gate_s[...] * pool_scale_ref[...]
    y_a = _dot(z.astype(BF16), w_a_ref[...])
    pool_out_ref[...] = ext_u[:, seg + POOL_HIST - POOL_ROWS:seg + POOL_HIST, :]
    ext_u[:, 0:POOL_HIST, :] = ext_u[:, seg:seg + POOL_HIST, :]

    g_a = jax.nn.sigmoid(proj_s[:, 3 * w:4 * w] + b_gate_ref[:, 0:w])
    m = g_a * y_a

    xb = proj_s[:, w:2 * w]
    ext_c[:, CONV_HIST:CONV_HIST + seg, :] = xb.reshape(bb, seg, w)
    xc = conv_b_ref[...] + xb * conv_w_ref[CONV_WIDTH - 1:CONV_WIDTH, :]
    for k in range(CONV_WIDTH - 1):
        sh = CONV_WIDTH - 1 - k
        xc = xc + ext_c[:, CONV_HIST - sh:CONV_HIST - sh + seg, :].reshape(rows, w) * conv_w_ref[k:k + 1, :]
    conv_out_ref[...] = ext_c[:, seg + CONV_HIST - CONV_ROWS:seg + CONV_HIST, :]
    ext_c[:, 0:CONV_HIST, :] = ext_c[:, seg:seg + CONV_HIST, :]

    xcb = xc.astype(BF16)
    lam = -lam_ref[...]
    softplus = jnp.maximum(lam, 0.0) + jnp.log1p(jnp.exp(-jnp.abs(lam)))
    for hh in range(LRU_HEADS):
        lo = hh * hdim
        ri = _dot(xcb[:, lo:lo + hdim], w_gi_ref[hh])
        r = jax.nn.sigmoid(ri[:, 0:hdim] + b_rg_ref[:, lo:lo + hdim])
        i = jax.nn.sigmoid(ri[:, hdim:2 * hdim] + b_ig_ref[:, lo:lo + hdim])
        log_a = (-LRU_C) * r * softplus[:, lo:lo + hdim]
        a = jnp.exp(log_a)
        mult = jnp.sqrt(-jnp.tanh(log_a) * (a * a + 1.0))
        mult = jnp.where(pos == 0, 1.0, mult)
        a_s[:, lo:lo + hdim] = a
        b_s[:, lo:lo + hdim] = mult * i * xc[:, lo:lo + hdim]

    tiles = rows // SUBLANES
    a = a_s[...].reshape(tiles, SUBLANES, w)
    b = b_s[...].reshape(tiles, SUBLANES, w)
    rin = lax.broadcasted_iota(I32, (1, SUBLANES, 1), 1)
    for d in (1, 2, 4):
        a_sh = jnp.where(rin >= d, pltpu.roll(a, d, 1), 1.0)
        b_sh = jnp.where(rin >= d, pltpu.roll(b, d, 1), 0.0)
        b = a * b_sh + b
        a = a * a_sh
    a = a.reshape(rows, w)
    b = b.reshape(rows, w)
    if seg == SUBLANES:
        hs = a * hcar[...] + b
        h_out_ref[...] = hs.reshape(bb, SUBLANES, w)[:, SUBLANES - 1:SUBLANES, :]
    else:
        a_s[...] = a
        b_s[...] = b

        def tile_step(k, hp):
            r0 = pl.multiple_of(k * SUBLANES, SUBLANES)
            ht = a_s[pl.ds(r0, SUBLANES), :] * hp + b_s[pl.ds(r0, SUBLANES), :]
            hs_s[pl.ds(r0, SUBLANES), :] = ht
            return jnp.broadcast_to(ht[SUBLANES - 1:SUBLANES, :], (SUBLANES, w))

        hp = lax.fori_loop(0, rows // SUBLANES, tile_step, hcar[...])
        hcar[...] = hp
        h_out_ref[...] = hp.reshape(bb, SUBLANES, w)[:, SUBLANES - 1:SUBLANES, :]
        hs = hs_s[...]

    yb = proj_s[:, 2 * w:3 * w]
    y_b = _dot((hs * _gelu(yb)).astype(BF16), w_b_ref[...])
    g_b = jax.nn.sigmoid(proj_s[:, 4 * w:5 * w] + b_gate_ref[:, w:2 * w])
    m = m + g_b * y_b
    out = _dot(m.astype(BF16), w_out_ref[...])
    x1_ref[...] = (x + _mod_rows(g1_ref, rows) * out).reshape(bb, seg, w)


def _const_spec(shape):
    nd = len(shape)
    return pl.BlockSpec(shape, lambda *_: (0,) * nd, pipeline_mode=pl.Buffered(1))


_MIXER_WEIGHTS = ("norm1", "w_in", "b_gate", "pool_mix", "pool_scale", "conv_w", "conv_b",
                  "w_gi", "b_rg", "b_ig", "lam", "w_a", "w_b", "w_out")
_MIXER_INPUTS = 7 + len(_MIXER_WEIGHTS)
_MIXER_OUTPUTS = 4


def _mixer_cast_kernel(*refs, ntab, **kw):
    ins = refs[:_MIXER_INPUTS + ntab]
    outs = refs[len(ins):len(ins) + _MIXER_OUTPUTS + ntab]
    scratch = refs[len(ins) + len(outs):]
    for k in range(ntab):
        outs[_MIXER_OUTPUTS + k][...] = ins[_MIXER_INPUTS + k][...].astype(BF16)
    _mixer_kernel(*ins[:_MIXER_INPUTS], *outs[:_MIXER_OUTPUTS], *scratch, **kw)


def _mixer(x, mod3, pool0, conv0, h0e, weights, *, bb, seg, pos0, tables=()):
    bsz, t, d = x.shape
    w = d
    rows = bb * seg
    nb, nt = bsz // bb, t // seg
    assert bb == 1 or nt == 1, "a tile is part of one sequence or holds whole sequences"
    steps = nb * nt
    tab_specs = [pl.BlockSpec((tb.shape[0] // steps, tb.shape[1]), lambda b, tt: (b * nt + tt, 0))
                 for tb in tables]
    assert all(tb.shape[0] % (steps * 2 * SUBLANES) == 0 for tb in tables)
    mod_specs = [pl.BlockSpec((bb, 1, d), functools.partial(lambda b, tt, col: (b, 0, col), col=col))
                 for col in range(3)]
    wvals = [weights[k] for k in _MIXER_WEIGHTS]
    kern = functools.partial(_mixer_cast_kernel, ntab=len(tables), bb=bb, seg=seg, pos0=pos0, width=w)
    return pl.pallas_call(
        kern,
        grid=(nb, nt),
        in_specs=[
            pl.BlockSpec((bb, seg, d), lambda b, tt: (b, tt, 0)),
            *mod_specs,
            pl.BlockSpec((bb, POOL_ROWS, w), lambda b, tt: (b, 0, 0)),
            pl.BlockSpec((bb, CONV_ROWS, w), lambda b, tt: (b, 0, 0)),
            pl.BlockSpec((bb, 1, w), lambda b, tt: (b, 0, 0)),
        ] + [_const_spec(v.shape) for v in wvals] + tab_specs,
        out_specs=[
            pl.BlockSpec((bb, seg, d), lambda b, tt: (b, tt, 0)),
            pl.BlockSpec((bb, POOL_ROWS, w), lambda b, tt: (b, 0, 0)),
            pl.BlockSpec((bb, CONV_ROWS, w), lambda b, tt: (b, 0, 0)),
            pl.BlockSpec((bb, 1, w), lambda b, tt: (b, 0, 0)),
        ] + tab_specs,
        out_shape=[
            jax.ShapeDtypeStruct((bsz, t, d), F32),
            jax.ShapeDtypeStruct((bsz, POOL_ROWS, w), F32),
            jax.ShapeDtypeStruct((bsz, CONV_ROWS, w), F32),
            jax.ShapeDtypeStruct((bsz, 1, w), F32),
        ] + [jax.ShapeDtypeStruct(tb.shape, BF16) for tb in tables],
        scratch_shapes=[
            pltpu.VMEM((bb, POOL_HIST + seg, w), F32),
            pltpu.VMEM((bb, CONV_HIST + seg, w), F32),
            pltpu.VMEM((rows, w), F32),
            pltpu.VMEM((rows, w), F32),
            pltpu.VMEM((rows, w), F32),
            pltpu.VMEM((rows, w), F32),
            pltpu.VMEM((bb * SUBLANES, w), F32),
            pltpu.VMEM((rows, weights["w_in"].shape[1]), F32),
        ],
        compiler_params=pltpu.CompilerParams(
            dimension_semantics=("arbitrary", "arbitrary"), vmem_limit_bytes=VMEM_LIMIT),
        name="mixer",
    )(x, mod3, mod3, mod3, pool0, conv0, h0e, *wvals, *tables)


def _sort16_network():
    n, pairs, p = 16, [], 1
    while p < n:
        k = p
        while k >= 1:
            for j in range(k % p, n - k, 2 * k):
                for i in range(min(k, n - j - k)):
                    if (i + j) // (2 * p) == (i + j + k) // (2 * p):
                        pairs.append((i + j, i + j + k))
            k //= 2
        p *= 2
    return pairs


def _top16_rows(s, val_ref, idx_ref):
    nt = s.shape[0] // SUBLANES
    assert nt == PEER_TOPK
    sub = lax.broadcasted_iota(I32, (SUBLANES, LANES), 0).astype(F32)
    v = [s[SUBLANES * k:SUBLANES * (k + 1), :] for k in range(nt)]
    ix = [sub + float(SUBLANES * k) for k in range(nt)]
    for i, j in _sort16_network():
        swap = (v[j] > v[i]) | ((v[j] == v[i]) & (ix[j] < ix[i]))
        v[i], v[j] = jnp.where(swap, v[j], v[i]), jnp.where(swap, v[i], v[j])
        ix[i], ix[j] = jnp.where(swap, ix[j], ix[i]), jnp.where(swap, ix[i], ix[j])
    for it in range(PEER_TOPK):
        mx = jnp.max(v[0], axis=0, keepdims=True)
        best = jnp.min(jnp.where(v[0] == mx, ix[0], float(s.shape[0])), axis=0, keepdims=True)
        val_ref[it:it + 1, :] = mx
        idx_ref[it:it + 1, :] = best
        win = ix[0] == best
        for r in range(PEER_TOPK - 1 - it):
            v[r] = jnp.where(win, v[r + 1], v[r])
            ix[r] = jnp.where(win, ix[r + 1], ix[r])


def _route_kernel(x_ref, sh_ref, sc_ref, norm2_ref, wq_ref, keys_ref,
                  h2_ref, e_ref, g_ref, h2_s, q_s, *list_refs, ntok):
    hd = pl.program_id(1)
    nsub = ntok // LANES
    lists = [list_refs[6 * u:6 * u + 6] for u in range(ROUTE_HEADS * nsub)]

    @pl.when(hd == 0)
    def _():
        x = x_ref[...]
        h2 = _rmsnorm(x, norm2_ref[...]) * (1.0 + _mod_rows(sc_ref, ntok)) + _mod_rows(sh_ref, ntok)
        h2b = h2.astype(BF16)
        h2_s[...] = h2b
        h2_ref[...] = h2b

    q_s[...] = _dot(h2_s[...], wq_ref[...])
    half = PEER_NKEYS

    sub = lax.broadcasted_iota(I32, (SUBLANES, LANES), 0)
    is_row = sub < 4
    nkf = float(PEER_NKEYS)
    nexp = nkf * nkf
    depth = jnp.full((SUBLANES, LANES), 0, I32)
    for r, dep in enumerate((16, 8, 5, 4, 12, 4, 1)):
        depth = jnp.where(sub == r, dep, depth)
    subf = sub.astype(F32)
    flat0 = jnp.where(is_row, subf * PEER_TOPK, 4.0 * PEER_TOPK + (subf - 4.0)) * nexp
    dflat = jnp.where(is_row, 1.0, float(PEER_TOPK)) * nexp

    def sub_block(hh, j):
        v1_s, i1_s, v2_s, i2_s, bs_s, bk_s = lists[hh * nsub + j]
        qj = q_s[j * LANES:(j + 1) * LANES, 2 * half * hh:2 * half * (hh + 1)].astype(BF16)
        _top16_rows(_dot_nt(keys_ref[hh, 0], qj[:, 0:half]), v1_s, i1_s)
        _top16_rows(_dot_nt(keys_ref[hh, 1], qj[:, half:2 * half]), v2_s, i2_s)
        vfix = jnp.where(is_row, v1_s[0:SUBLANES, :], pltpu.roll(v2_s[0:SUBLANES, :], 4, 0))
        efix = jnp.where(is_row, i1_s[0:SUBLANES, :] * nkf, pltpu.roll(i2_s[0:SUBLANES, :], 4, 0))
        t, k = [], []
        for d in range(PEER_TOPK):
            if d + 4 < PEER_TOPK:
                vrun = jnp.where(is_row, v2_s[d:d + 1, :], v1_s[d + 4:d + 5, :])
                erun = jnp.where(is_row, i2_s[d:d + 1, :], i1_s[d + 4:d + 5, :] * nkf)
            else:
                vrun = v2_s[d:d + 1, :]
                erun = i2_s[d:d + 1, :]
            t.append(jnp.where(depth > d, vfix + vrun, -jnp.inf))
            k.append(flat0 + dflat * float(d) + (efix + erun))
        for it in range(PEER_TOPK):
            mx = jnp.max(t[0], axis=0, keepdims=True)
            kx = jnp.min(jnp.where(t[0] == mx, k[0], PEER_TOPK * PEER_TOPK * nexp), axis=0, keepdims=True)
            bk_s[it:it + 1, :] = kx
            bs_s[it:it + 1, :] = mx
            win = k[0] == kx
            for r in range(PEER_TOPK - 1 - it):
                t[r] = jnp.where(win, t[r + 1], t[r])
                k[r] = jnp.where(win, k[r + 1], k[r])
        best = bs_s[...]
        ex = jnp.exp(best - best[0:1, :])
        gsm = ex / jnp.sum(ex, axis=0, keepdims=True)
        bk = bk_s[...]
        rows = slice(hh * PEER_TOPK, (hh + 1) * PEER_TOPK)
        e_ref[j, rows, :] = bk - jnp.floor(bk / nexp) * nexp
        g_ref[j, rows, :] = gsm

    for hh in range(ROUTE_HEADS):
        for j in range(nsub):
            sub_block(hh, j)


def _route(x1, mod3, norm2, w_q, keys, *, seq_tokens):
    n, d = x1.shape
    nt = ROUTE_TOK
    tiles = n // nt
    mod_specs = [_mod_spec(seq_tokens, nt, d, col, lambda i, hd: i) for col in (3, 4)]
    dk2 = w_q.shape[1] // PEER_HEADS
    rh = ROUTE_HEADS
    kern = functools.partial(_route_kernel, ntok=nt)
    return pl.pallas_call(
        kern,
        grid=(tiles, PEER_HEADS // rh),
        in_specs=[
            pl.BlockSpec((nt, d), lambda i, hd: (i, 0)),
            *mod_specs,
            pl.BlockSpec((1, d), lambda i, hd: (0, 0)),
            pl.BlockSpec((d, rh * dk2), lambda i, hd: (0, hd)),
            pl.BlockSpec((rh, 2, PEER_NKEYS, dk2 // 2), lambda i, hd: (hd, 0, 0, 0)),
        ],
        out_specs=[
            pl.BlockSpec((nt, d), lambda i, hd: (i, 0)),
            pl.BlockSpec((nt // LANES, rh * PEER_TOPK, LANES), lambda i, hd: (i, hd, 0)),
            pl.BlockSpec((nt // LANES, rh * PEER_TOPK, LANES), lambda i, hd: (i, hd, 0)),
        ],
        out_shape=[
            jax.ShapeDtypeStruct((n, d), BF16),
            jax.ShapeDtypeStruct((n // LANES, PEER_HEADS * PEER_TOPK, LANES), F32),
            jax.ShapeDtypeStruct((n // LANES, PEER_HEADS * PEER_TOPK, LANES), F32),
        ],
        scratch_shapes=[
            pltpu.VMEM((nt, d), BF16),
            pltpu.VMEM((nt, rh * dk2), F32),
        ] + [pltpu.VMEM((PEER_TOPK, LANES), F32)] * (6 * rh * (nt // LANES)),
        compiler_params=pltpu.CompilerParams(
            dimension_semantics=("arbitrary", "arbitrary"), vmem_limit_bytes=VMEM_LIMIT),
        name="route",
    )(x1, mod3, mod3, norm2, w_q, keys)


def _expert_kernel(h2_ref, e_ref, g_ref, u_ref, v_ref, x1_ref, g2_ref, fnorm_ref,
                   y_ref, s_s, erow_s, grow_s, w2_s, acc_s, *, ntok):
    c = pl.program_id(1)
    nchunks = pl.num_programs(1)
    nk = PEER_NKEYS

    @pl.when(c == 0)
    def _():
        acc_s[...] = jnp.zeros_like(acc_s)
        for sb in range(ntok // LANES):
            erow_s[sb * LANES:(sb + 1) * LANES, :] = e_ref[sb].T.astype(I32)
            grow_s[sb * LANES:(sb + 1) * LANES, :] = g_ref[sb].T
        riota = lax.broadcasted_iota(I32, (nk, nk), 0)
        riota2 = lax.broadcasted_iota(I32, (2 * nk, nk), 0)

        def pair(p):
            n0 = pl.multiple_of(2 * p, 2)
            lts, rts = [], []
            for t in range(2):
                e = erow_s[pl.ds(n0 + t, 1), :]
                gv = grow_s[pl.ds(n0 + t, 1), :]
                i1 = e >> 7
                i2 = e & (nk - 1)
                lts.append(jnp.where(riota2 == 2 * i1 + t, gv, 0.0).astype(BF16))
                rts.append(jnp.where(riota == i2, 1.0, 0.0).astype(BF16))
            x = _dot_nt(jnp.concatenate(lts, axis=1), jnp.concatenate(rts, axis=1))
            words = pltpu.bitcast(x.astype(BF16), I32)
            s_s[pl.ds(pl.multiple_of(p * S_PITCH, SUBLANES), nk), :] = words

        def pair_group(pg, carry):
            for q in range(PAIR_UNROLL):
                pair(pg * PAIR_UNROLL + q)
            return carry

        lax.fori_loop(0, ntok // (2 * PAIR_UNROLL), pair_group, 0)

    h2 = h2_ref[...]
    i1_per_sub = EXP_SUB // nk
    for j in range(EXP_CHUNK // EXP_SUB):
        a = _dot_nt(h2, u_ref[j * EXP_SUB:(j + 1) * EXP_SUB, :])
        i1_0 = c * (EXP_CHUNK // nk) + j * i1_per_sub
        gs = [pltpu.bitcast(s_s[pl.ds(i1_0 + t, ntok // 2, stride=S_PITCH), :], BF16)
              for t in range(i1_per_sub)]
        g = jnp.concatenate(gs, axis=1).astype(F32)
        w2_s[:, j * EXP_SUB:(j + 1) * EXP_SUB] = (_gelu(a) * g).astype(BF16)
    acc_s[...] += _dot(w2_s[...], v_ref[...])

    @pl.when(c == nchunks - 1)
    def _():
        x2 = x1_ref[...] + _mod_rows(g2_ref, ntok) * acc_s[...]
        y_ref[...] = _rmsnorm(x2, fnorm_ref[...])


def _expert(h2, eT, gT, u_bf, v_bf, x1, mod3, fnorm, *, seq_tokens):
    n, d = x1.shape
    nt = EXP_TOK
    ne = u_bf.shape[0]
    kern = functools.partial(_expert_kernel, ntok=nt)
    return pl.pallas_call(
        kern,
        grid=(n // nt, ne // EXP_CHUNK),
        in_specs=[
            pl.BlockSpec((nt, d), lambda i, c: (i, 0)),
            pl.BlockSpec((nt // LANES, PEER_HEADS * PEER_TOPK, LANES), lambda i, c: (i, 0, 0)),
            pl.BlockSpec((nt // LANES, PEER_HEADS * PEER_TOPK, LANES), lambda i, c: (i, 0, 0)),
            pl.BlockSpec((EXP_CHUNK, d), lambda i, c: (c, 0)),
            pl.BlockSpec((EXP_CHUNK, d), lambda i, c: (c, 0)),
            pl.BlockSpec((nt, d), lambda i, c: (i, 0)),
            _mod_spec(seq_tokens, nt, d, 5, lambda i, c: i),
            pl.BlockSpec((1, d), lambda i, c: (0, 0)),
        ],
        out_specs=pl.BlockSpec((nt, d), lambda i, c: (i, 0)),
        out_shape=jax.ShapeDtypeStruct((n, d), F32),
        scratch_shapes=[
            pltpu.VMEM((nt // 2 * S_PITCH, LANES), I32),
            pltpu.VMEM((nt, PEER_HEADS * PEER_TOPK), I32),
            pltpu.VMEM((nt, PEER_HEADS * PEER_TOPK), F32),
            pltpu.VMEM((nt, EXP_CHUNK), BF16),
            pltpu.VMEM((nt, d), F32),
        ],
        compiler_params=pltpu.CompilerParams(
            dimension_semantics=("arbitrary", "arbitrary"), vmem_limit_bytes=VMEM_LIMIT),
        name="expert",
    )(h2, eT, gT, u_bf, v_bf, x1, mod3, fnorm)


def _run_group(x, mod, pool0, conv0, h0, wts, expert_tables, *, bb, seg, pos0):
    bsz, t, d = x.shape
    n = bsz * t
    mod3 = mod[:, None, :]
    cast = tuple(tb for tb in expert_tables if tb.dtype != BF16)
    x1, pool_o, conv_o, h_o, *cast_out = _mixer(
        x, mod3, pool0, conv0, h0[:, None, :], wts, bb=bb, seg=seg, pos0=pos0, tables=cast)
    u_bf, v_bf = cast_out if cast else expert_tables
    x1f = x1.reshape(n, d)
    h2, eT, gT = _route(x1f, mod3, wts["norm2"], wts["w_q"], wts["keys"], seq_tokens=t)
    y = _expert(h2, eT, gT, u_bf, v_bf, x1f, mod3, wts["fnorm"], seq_tokens=t)
    return (y.reshape(bsz, t, d), pool_o, conv_o, h_o[:, 0]), (u_bf, v_bf)


def kernel(x_prompt, x_sample, c_prompt, c_sample, state_pool, state_conv, state_lru, norm1, norm2, w_ada, b_ada, w_in, b_gate, pool_mix, pool_scale, conv_w, conv_b, w_rg, b_rg, w_ig, b_ig, lru_lambda, w_branch_a, w_branch_b, w_out, w_q, sub_keys, expert_u, expert_v, final_norm):
    depth = w_in.shape[0]
    assert depth == 1, "single-layer trunk"
    bp, tp, d = x_prompt.shape
    bs, ts, _ = x_sample.shape
    l = 0
    mod = _ada(jnp.concatenate([c_prompt, c_sample], axis=0), w_ada[l], b_ada[l])
    row = lambda v: v.reshape(1, -1)
    wts = {
        "norm1": row(norm1[l]), "norm2": row(norm2[l]), "fnorm": row(final_norm),
        "w_in": w_in[l].astype(BF16), "b_gate": row(b_gate[l]),
        "pool_mix": pool_mix[l].astype(BF16), "pool_scale": row(pool_scale[l]),
        "conv_w": conv_w[l], "conv_b": row(conv_b[l]),
        "w_gi": jnp.concatenate([w_rg[l], w_ig[l]], axis=-1).astype(BF16),
        "b_rg": row(b_rg[l]), "b_ig": row(b_ig[l]), "lam": row(lru_lambda[l]),
        "w_a": w_branch_a[l].astype(BF16), "w_b": w_branch_b[l].astype(BF16),
        "w_out": w_out[l].astype(BF16), "w_q": w_q[l].astype(BF16),
        "keys": sub_keys[l].astype(BF16),
    }
    zp = jnp.zeros((bp, state_pool.shape[2], state_pool.shape[3]), F32)
    zc = jnp.zeros((bp, state_conv.shape[2], state_conv.shape[3]), F32)
    zh = jnp.zeros((bp, state_lru.shape[2]), F32)
    (y_p, pool_p, conv_p, lru_p), tables_bf = _run_group(
        x_prompt, mod[:bp], zp, zc, zh, wts, (expert_u[l], expert_v[l]), bb=1, seg=256, pos0=0)
    (y_s, pool_s, conv_s, lru_s), _ = _run_group(
        x_sample, mod[bp:], state_pool[l], state_conv[l], state_lru[l], wts, tables_bf,
        bb=32, seg=ts, pos0=PAST_LEN)
    st = lambda v: v[None]
    return (y_p, y_s, st(pool_p), st(conv_p), st(lru_p).astype(state_lru.dtype),
            st(pool_s), st(conv_s), st(lru_s).astype(state_lru.dtype))
```

```python
import functools

import jax
import jax.numpy as jnp
from jax import lax
from jax.experimental import pallas as pl
from jax.experimental.pallas import tpu as pltpu

F32 = jnp.float32
BF16 = jnp.bfloat16
I32 = jnp.int32

NORM_EPS = 1e-6
PAST_LEN = 16384
LRU_C = 8.0
POOL_WINDOWS = (2, 4, 8, 16)
POOL_ROWS = max(POOL_WINDOWS) - 1
POOL_HIST = 16
CONV_WIDTH = 4
CONV_ROWS = CONV_WIDTH - 1
CONV_HIST = 8
LRU_HEADS = 8
PEER_HEADS = 8
PEER_NKEYS = 128
PEER_TOPK = 16
SUBLANES = 8
LANES = 128
VMEM_LIMIT = 56 * 1024 * 1024

EXP_TOK = 512
EXP_CHUNK = 2048
EXP_SUB = 256
PAIR_UNROLL = 64
S_PITCH = 136
ROUTE_TOK = 1024
ROUTE_HEADS = 4


def _dot(a, b):
    return jnp.dot(a, b, preferred_element_type=F32)


def _dot_nt(a, b):
    return lax.dot_general(a, b, (((1,), (1,)), ((), ())), preferred_element_type=F32)


def _rmsnorm(x, g):
    return x * lax.rsqrt(jnp.mean(x * x, axis=-1, keepdims=True) + NORM_EPS) * g


def _gelu(x):
    return jax.nn.gelu(x, approximate=True)


def _mod_rows(m_ref, rows):
    m = m_ref[...]
    nseq, _, d = m.shape
    if nseq == 1:
        return m[0]
    return jnp.broadcast_to(m, (nseq, rows // nseq, d)).reshape(rows, d)


def _mod_spec(seq_tokens, tile_tokens, d, col, tile_of):
    if seq_tokens >= tile_tokens:
        tiles_per_seq = seq_tokens // tile_tokens
        return pl.BlockSpec((1, 1, d), lambda *g: (tile_of(*g) // tiles_per_seq, 0, col))
    return pl.BlockSpec((tile_tokens // seq_tokens, 1, d), lambda *g: (tile_of(*g), 0, col))


def _ada_kernel(cp_ref, cs_ref, w_ref, b_ref, op_ref, os_ref):
    wb = w_ref[...].astype(BF16)
    for c_ref, o_ref in ((cp_ref, op_ref), (cs_ref, os_ref)):
        c = c_ref[...]
        s = (c * jax.nn.sigmoid(c)).astype(BF16)
        o_ref[...] = _dot(s, wb) + b_ref[...]


def _ada(c_p, c_s, w_ada, b_ada):
    d = c_p.shape[1]
    dout = w_ada.shape[1]
    tn = 1024
    return pl.pallas_call(
        _ada_kernel,
        grid=(dout // tn,),
        in_specs=[
            pl.BlockSpec(c_p.shape, lambda j: (0, 0)),
            pl.BlockSpec(c_s.shape, lambda j: (0, 0)),
            pl.BlockSpec((d, tn), lambda j: (0, j)),
            pl.BlockSpec((1, tn), lambda j: (0, j)),
        ],
        out_specs=[pl.BlockSpec((c_p.shape[0], tn), lambda j: (0, j)),
                   pl.BlockSpec((c_s.shape[0], tn), lambda j: (0, j))],
        out_shape=[jax.ShapeDtypeStruct((c_p.shape[0], dout), F32),
                   jax.ShapeDtypeStruct((c_s.shape[0], dout), F32)],
        compiler_params=pltpu.CompilerParams(
            dimension_semantics=("arbitrary",), vmem_limit_bytes=VMEM_LIMIT),
        name="ada",
    )(c_p, c_s, w_ada, b_ada.reshape(1, dout))


def _mixer_kernel(x_ref, sh_ref, sc_ref, g1_ref, pool0_ref, conv0_ref, h0_ref,
                  norm1_ref, w_in_ref, b_gate_ref, pool_mix_ref, pool_scale_ref,
                  conv_w_ref, conv_b_ref, w_gi_ref, b_rg_ref, b_ig_ref, lam_ref,
                  w_a_ref, w_b_ref, w_out_ref,
                  x1_ref, pool_out_ref, conv_out_ref, h_out_ref,
                  ext_u, ext_c, gate_s, a_s, b_s, hs_s, hcar, proj_s,
                  *, bb, seg, pos0, width):
    rows = bb * seg
    tt = pl.program_id(1)
    w = width
    gdim = w // len(POOL_WINDOWS)
    hdim = w // LRU_HEADS

    @pl.when(tt == 0)
    def _():
        ext_u[:, 0:POOL_HIST, :] = jnp.zeros((bb, POOL_HIST, w), F32)
        ext_u[:, POOL_HIST - POOL_ROWS:POOL_HIST, :] = pool0_ref[...]
        ext_c[:, 0:CONV_HIST, :] = jnp.zeros((bb, CONV_HIST, w), F32)
        ext_c[:, CONV_HIST - CONV_ROWS:CONV_HIST, :] = conv0_ref[...]
        hcar[...] = jnp.broadcast_to(h0_ref[...], (bb, SUBLANES, w)).reshape(bb * SUBLANES, w)

    x = x_ref[...].reshape(rows, w)
    h = _rmsnorm(x, norm1_ref[...]) * (1.0 + _mod_rows(sc_ref, rows)) + _mod_rows(sh_ref, rows)
    hb = h.astype(BF16)

    row = lax.broadcasted_iota(I32, (rows, 1), 0)
    pos = pos0 + tt * seg + (row & (seg - 1))

    proj_s[...] = _dot(hb, w_in_ref[...])
    u = proj_s[:, 0:w]
    ext_u[:, POOL_HIST:POOL_HIST + seg, :] = u.reshape(bb, seg, w)
    ext_rows = bb * (POOL_HIST + seg)
    wsum = ext_u[:, :, 0:w].reshape(ext_rows, w)
    level = 1
    for g, win in enumerate(POOL_WINDOWS):
        lo = g * gdim
        while level < win:
            wsum = wsum + pltpu.roll(wsum, level, 0)
            level *= 2
        acc = wsum[:, 0:gdim].reshape(bb, POOL_HIST + seg, gdim)[:, POOL_HIST:, :].reshape(rows, gdim)
        wsum = wsum[:, gdim:]
        cnt = jnp.minimum(pos + 1, win).astype(F32)
        pooled = acc / cnt - u[:, lo:lo + gdim]
        zg = _dot(pooled.astype(BF16), pool_mix_ref[g])
        gate_s[:, lo:lo + gdim] = zg
    z = gate_s[...] * pool_scale_ref[...]
    y_a = _dot(z.astype(BF16), w_a_ref[...])
    pool_out_ref[...] = ext_u[:, seg + POOL_HIST - POOL_ROWS:seg + POOL_HIST, :]
    ext_u[:, 0:POOL_HIST, :] = ext_u[:, seg:seg + POOL_HIST, :]

    g_a = jax.nn.sigmoid(proj_s[:, 3 * w:4 * w] + b_gate_ref[:, 0:w])
    m = g_a * y_a

    xb = proj_s[:, w:2 * w]
    ext_c[:, CONV_HIST:CONV_HIST + seg, :] = xb.reshape(bb, seg, w)
    xc = conv_b_ref[...] + xb * conv_w_ref[CONV_WIDTH - 1:CONV_WIDTH, :]
    for k in range(CONV_WIDTH - 1):
        sh = CONV_WIDTH - 1 - k
        xc = xc + ext_c[:, CONV_HIST - sh:CONV_HIST - sh + seg, :].reshape(rows, w) * conv_w_ref[k:k + 1, :]
    conv_out_ref[...] = ext_c[:, seg + CONV_HIST - CONV_ROWS:seg + CONV_HIST, :]
    ext_c[:, 0:CONV_HIST, :] = ext_c[:, seg:seg + CONV_HIST, :]

    xcb = xc.astype(BF16)
    lam = -lam_ref[...]
    softplus = jnp.maximum(lam, 0.0) + jnp.log1p(jnp.exp(-jnp.abs(lam)))
    for hh in range(LRU_HEADS):
        lo = hh * hdim
        ri = _dot(xcb[:, lo:lo + hdim], w_gi_ref[hh])
        r = jax.nn.sigmoid(ri[:, 0:hdim] + b_rg_ref[:, lo:lo + hdim])
        i = jax.nn.sigmoid(ri[:, hdim:2 * hdim] + b_ig_ref[:, lo:lo + hdim])
        log_a = (-LRU_C) * r * softplus[:, lo:lo + hdim]
        a = jnp.exp(log_a)
        mult = jnp.sqrt(-jnp.tanh(log_a) * (a * a + 1.0))
        mult = jnp.where(pos == 0, 1.0, mult)
        a_s[:, lo:lo + hdim] = a
        b_s[:, lo:lo + hdim] = mult * i * xc[:, lo:lo + hdim]

    tiles = rows // SUBLANES
    a = a_s[...].reshape(tiles, SUBLANES, w)
    b = b_s[...].reshape(tiles, SUBLANES, w)
    rin = lax.broadcasted_iota(I32, (1, SUBLANES, 1), 1)
    for d in (1, 2, 4):
        a_sh = jnp.where(rin >= d, pltpu.roll(a, d, 1), 1.0)
        b_sh = jnp.where(rin >= d, pltpu.roll(b, d, 1), 0.0)
        b = a * b_sh + b
        a = a * a_sh
    a = a.reshape(rows, w)
    b = b.reshape(rows, w)
    if seg == SUBLANES:
        hs = a * hcar[...] + b
        h_out_ref[...] = hs.reshape(bb, SUBLANES, w)[:, SUBLANES - 1:SUBLANES, :]
    else:
        a_s[...] = a
        b_s[...] = b

        def tile_step(k, hp):
            r0 = pl.multiple_of(k * SUBLANES, SUBLANES)
            ht = a_s[pl.ds(r0, SUBLANES), :] * hp + b_s[pl.ds(r0, SUBLANES), :]
            hs_s[pl.ds(r0, SUBLANES), :] = ht
            return jnp.broadcast_to(ht[SUBLANES - 1:SUBLANES, :], (SUBLANES, w))

        hp = lax.fori_loop(0, rows // SUBLANES, tile_step, hcar[...])
        hcar[...] = hp
        h_out_ref[...] = hp.reshape(bb, SUBLANES, w)[:, SUBLANES - 1:SUBLANES, :]
        hs = hs_s[...]

    yb = proj_s[:, 2 * w:3 * w]
    y_b = _dot((hs * _gelu(yb)).astype(BF16), w_b_ref[...])
    g_b = jax.nn.sigmoid(proj_s[:, 4 * w:5 * w] + b_gate_ref[:, w:2 * w])
    m = m + g_b * y_b
    out = _dot(m.astype(BF16), w_out_ref[...])
    x1_ref[...] = (x + _mod_rows(g1_ref, rows) * out).reshape(bb, seg, w)


def _const_spec(shape):
    nd = len(shape)
    return pl.BlockSpec(shape, lambda *_: (0,) * nd, pipeline_mode=pl.Buffered(1))


_MIXER_WEIGHTS = ("norm1", "w_in", "b_gate", "pool_mix", "pool_scale", "conv_w", "conv_b",
                  "w_gi", "b_rg", "b_ig", "lam", "w_a", "w_b", "w_out")
_MIXER_INPUTS = 7 + len(_MIXER_WEIGHTS)
_MIXER_OUTPUTS = 4


def _mixer_cast_kernel(*refs, ntab, **kw):
    ins = refs[:_MIXER_INPUTS + ntab]
    outs = refs[len(ins):len(ins) + _MIXER_OUTPUTS + ntab]
    scratch = refs[len(ins) + len(outs):]
    for k in range(ntab):
        outs[_MIXER_OUTPUTS + k][...] = ins[_MIXER_INPUTS + k][...].astype(BF16)
    _mixer_kernel(*ins[:_MIXER_INPUTS], *outs[:_MIXER_OUTPUTS], *scratch, **kw)


def _mixer(x, mod3, pool0, conv0, h0e, weights, *, bb, seg, pos0, tables=()):
    bsz, t, d = x.shape
    w = d
    rows = bb * seg
    nb, nt = bsz // bb, t // seg
    assert bb == 1 or nt == 1, "a tile is part of one sequence or holds whole sequences"
    steps = nb * nt
    tab_specs = [pl.BlockSpec((tb.shape[0] // steps, tb.shape[1]), lambda b, tt: (b * nt + tt, 0))
                 for tb in tables]
    assert all(tb.shape[0] % (steps * 2 * SUBLANES) == 0 for tb in tables)
    mod_specs = [pl.BlockSpec((bb, 1, d), functools.partial(lambda b, tt, col: (b, 0, col), col=col))
                 for col in range(3)]
    wvals = [weights[k] for k in _MIXER_WEIGHTS]
    kern = functools.partial(_mixer_cast_kernel, ntab=len(tables), bb=bb, seg=seg, pos0=pos0, width=w)
    return pl.pallas_call(
        kern,
        grid=(nb, nt),
        in_specs=[
            pl.BlockSpec((bb, seg, d), lambda b, tt: (b, tt, 0)),
            *mod_specs,
            pl.BlockSpec((bb, POOL_ROWS, w), lambda b, tt: (b, 0, 0)),
            pl.BlockSpec((bb, CONV_ROWS, w), lambda b, tt: (b, 0, 0)),
            pl.BlockSpec((bb, 1, w), lambda b, tt: (b, 0, 0)),
        ] + [_const_spec(v.shape) for v in wvals] + tab_specs,
        out_specs=[
            pl.BlockSpec((bb, seg, d), lambda b, tt: (b, tt, 0)),
            pl.BlockSpec((bb, POOL_ROWS, w), lambda b, tt: (b, 0, 0)),
            pl.BlockSpec((bb, CONV_ROWS, w), lambda b, tt: (b, 0, 0)),
            pl.BlockSpec((bb, 1, w), lambda b, tt: (b, 0, 0)),
        ] + tab_specs,
        out_shape=[
            jax.ShapeDtypeStruct((bsz, t, d), F32),
            jax.ShapeDtypeStruct((bsz, POOL_ROWS, w), F32),
            jax.ShapeDtypeStruct((bsz, CONV_ROWS, w), F32),
            jax.ShapeDtypeStruct((bsz, 1, w), F32),
        ] + [jax.ShapeDtypeStruct(tb.shape, BF16) for tb in tables],
        scratch_shapes=[
            pltpu.VMEM((bb, POOL_HIST + seg, w), F32),
            pltpu.VMEM((bb, CONV_HIST + seg, w), F32),
            pltpu.VMEM((rows, w), F32),
            pltpu.VMEM((rows, w), F32),
            pltpu.VMEM((rows, w), F32),
            pltpu.VMEM((rows, w), F32),
            pltpu.VMEM((bb * SUBLANES, w), F32),
            pltpu.VMEM((rows, weights["w_in"].shape[1]), F32),
        ],
        compiler_params=pltpu.CompilerParams(
            dimension_semantics=("arbitrary", "arbitrary"), vmem_limit_bytes=VMEM_LIMIT),
        name="mixer",
    )(x, mod3, mod3, mod3, pool0, conv0, h0e, *wvals, *tables)


def _sort16_network():
    n, pairs, p = 16, [], 1
    while p < n:
        k = p
        while k >= 1:
            for j in range(k % p, n - k, 2 * k):
                for i in range(min(k, n - j - k)):
                    if (i + j) // (2 * p) == (i + j + k) // (2 * p):
                        pairs.append((i + j, i + j + k))
            k //= 2
        p *= 2
    return pairs


def _top16_rows(s, val_ref, idx_ref):
    nt = s.shape[0] // SUBLANES
    assert nt == PEER_TOPK
    sub = lax.broadcasted_iota(I32, (SUBLANES, LANES), 0).astype(F32)
    v = [s[SUBLANES * k:SUBLANES * (k + 1), :] for k in range(nt)]
    ix = [sub + float(SUBLANES * k) for k in range(nt)]
    for i, j in _sort16_network():
        swap = (v[j] > v[i]) | ((v[j] == v[i]) & (ix[j] < ix[i]))
        v[i], v[j] = jnp.where(swap, v[j], v[i]), jnp.where(swap, v[i], v[j])
        ix[i], ix[j] = jnp.where(swap, ix[j], ix[i]), jnp.where(swap, ix[i], ix[j])
    for it in range(PEER_TOPK):
        mx = jnp.max(v[0], axis=0, keepdims=True)
        best = jnp.min(jnp.where(v[0] == mx, ix[0], float(s.shape[0])), axis=0, keepdims=True)
        val_ref[it:it + 1, :] = mx
        idx_ref[it:it + 1, :] = best
        win = ix[0] == best
        for r in range(PEER_TOPK - 1 - it):
            v[r] = jnp.where(win, v[r + 1], v[r])
            ix[r] = jnp.where(win, ix[r + 1], ix[r])


def _route_kernel(x_ref, sh_ref, sc_ref, norm2_ref, wq_ref, keys_ref,
                  h2_ref, e_ref, g_ref, h2_s, q_s, *list_refs, ntok):
    hd = pl.program_id(1)
    nsub = ntok // LANES
    lists = [list_refs[6 * u:6 * u + 6] for u in range(ROUTE_HEADS * nsub)]

    @pl.when(hd == 0)
    def _():
        x = x_ref[...]
        h2 = _rmsnorm(x, norm2_ref[...]) * (1.0 + _mod_rows(sc_ref, ntok)) + _mod_rows(sh_ref, ntok)
        h2b = h2.astype(BF16)
        h2_s[...] = h2b
        h2_ref[...] = h2b

    q_s[...] = _dot(h2_s[...], wq_ref[...])
    half = PEER_NKEYS

    sub = lax.broadcasted_iota(I32, (SUBLANES, LANES), 0)
    is_row = sub < 4
    nkf = float(PEER_NKEYS)
    nexp = nkf * nkf
    depth = jnp.full((SUBLANES, LANES), 0, I32)
    for r, dep in enumerate((16, 8, 5, 4, 12, 4, 1)):
        depth = jnp.where(sub == r, dep, depth)
    subf = sub.astype(F32)
    flat0 = jnp.where(is_row, subf * PEER_TOPK, 4.0 * PEER_TOPK + (subf - 4.0)) * nexp
    dflat = jnp.where(is_row, 1.0, float(PEER_TOPK)) * nexp

    def sub_block(hh, j):
        v1_s, i1_s, v2_s, i2_s, bs_s, bk_s = lists[hh * nsub + j]
        qj = q_s[j * LANES:(j + 1) * LANES, 2 * half * hh:2 * half * (hh + 1)].astype(BF16)
        _top16_rows(_dot_nt(keys_ref[hh, 0], qj[:, 0:half]), v1_s, i1_s)
        _top16_rows(_dot_nt(keys_ref[hh, 1], qj[:, half:2 * half]), v2_s, i2_s)
        vfix = jnp.where(is_row, v1_s[0:SUBLANES, :], pltpu.roll(v2_s[0:SUBLANES, :], 4, 0))
        efix = jnp.where(is_row, i1_s[0:SUBLANES, :] * nkf, pltpu.roll(i2_s[0:SUBLANES, :], 4, 0))
        t, k = [], []
        for d in range(PEER_TOPK):
            if d + 4 < PEER_TOPK:
                vrun = jnp.where(is_row, v2_s[d:d + 1, :], v1_s[d + 4:d + 5, :])
                erun = jnp.where(is_row, i2_s[d:d + 1, :], i1_s[d + 4:d + 5, :] * nkf)
            else:
                vrun = v2_s[d:d + 1, :]
                erun = i2_s[d:d + 1, :]
            t.append(jnp.where(depth > d, vfix + vrun, -jnp.inf))
            k.append(flat0 + dflat * float(d) + (efix + erun))
        for it in range(PEER_TOPK):
            mx = jnp.max(t[0], axis=0, keepdims=True)
            kx = jnp.min(jnp.where(t[0] == mx, k[0], PEER_TOPK * PEER_TOPK * nexp), axis=0, keepdims=True)
            bk_s[it:it + 1, :] = kx
            bs_s[it:it + 1, :] = mx
            win = k[0] == kx
            for r in range(PEER_TOPK - 1 - it):
                t[r] = jnp.where(win, t[r + 1], t[r])
                k[r] = jnp.where(win, k[r + 1], k[r])
        best = bs_s[...]
        ex = jnp.exp(best - best[0:1, :])
        gsm = ex / jnp.sum(ex, axis=0, keepdims=True)
        bk = bk_s[...]
        rows = slice(hh * PEER_TOPK, (hh + 1) * PEER_TOPK)
        e_ref[j, rows, :] = bk - jnp.floor(bk / nexp) * nexp
        g_ref[j, rows, :] = gsm

    for hh in range(ROUTE_HEADS):
        for j in range(nsub):
            sub_block(hh, j)


def _route(x1, mod3, norm2, w_q, keys, *, seq_tokens):
    n, d = x1.shape
    nt = ROUTE_TOK
    tiles = n // nt
    mod_specs = [_mod_spec(seq_tokens, nt, d, col, lambda i, hd: i) for col in (3, 4)]
    dk2 = w_q.shape[1] // PEER_HEADS
    rh = ROUTE_HEADS
    kern = functools.partial(_route_kernel, ntok=nt)
    return pl.pallas_call(
        kern,
        grid=(tiles, PEER_HEADS // rh),
        in_specs=[
            pl.BlockSpec((nt, d), lambda i, hd: (i, 0)),
            *mod_specs,
            pl.BlockSpec((1, d), lambda i, hd: (0, 0)),
            pl.BlockSpec((d, rh * dk2), lambda i, hd: (0, hd)),
            pl.BlockSpec((rh, 2, PEER_NKEYS, dk2 // 2), lambda i, hd: (hd, 0, 0, 0)),
        ],
        out_specs=[
            pl.BlockSpec((nt, d), lambda i, hd: (i, 0)),
            pl.BlockSpec((nt // LANES, rh * PEER_TOPK, LANES), lambda i, hd: (i, hd, 0)),
            pl.BlockSpec((nt // LANES, rh * PEER_TOPK, LANES), lambda i, hd: (i, hd, 0)),
        ],
        out_shape=[
            jax.ShapeDtypeStruct((n, d), BF16),
            jax.ShapeDtypeStruct((n // LANES, PEER_HEADS * PEER_TOPK, LANES), F32),
            jax.ShapeDtypeStruct((n // LANES, PEER_HEADS * PEER_TOPK, LANES), F32),
        ],
        scratch_shapes=[
            pltpu.VMEM((nt, d), BF16),
            pltpu.VMEM((nt, rh * dk2), F32),
        ] + [pltpu.VMEM((PEER_TOPK, LANES), F32)] * (6 * rh * (nt // LANES)),
        compiler_params=pltpu.CompilerParams(
            dimension_semantics=("arbitrary", "arbitrary"), vmem_limit_bytes=VMEM_LIMIT),
        name="route",
    )(x1, mod3, mod3, norm2, w_q, keys)


def _expert_kernel(h2_ref, e_ref, g_ref, u_ref, v_ref, x1_ref, g2_ref, fnorm_ref,
                   y_ref, s_s, erow_s, grow_s, w2_s, acc_s, *, ntok):
    c = pl.program_id(1)
    nchunks = pl.num_programs(1)
    nk = PEER_NKEYS

    @pl.when(c == 0)
    def _():
        acc_s[...] = jnp.zeros_like(acc_s)
        for sb in range(ntok // LANES):
            erow_s[sb * LANES:(sb + 1) * LANES, :] = e_ref[sb].T.astype(I32)
            grow_s[sb * LANES:(sb + 1) * LANES, :] = g_ref[sb].T
        riota = lax.broadcasted_iota(I32, (nk, nk), 0)
        riota2 = lax.broadcasted_iota(I32, (2 * nk, nk), 0)

        def pair(p):
            n0 = pl.multiple_of(2 * p, 2)
            lts, rts = [], []
            for t in range(2):
                e = erow_s[pl.ds(n0 + t, 1), :]
                gv = grow_s[pl.ds(n0 + t, 1), :]
                i1 = e >> 7
                i2 = e & (nk - 1)
                lts.append(jnp.where(riota2 == 2 * i1 + t, gv, 0.0).astype(BF16))
                rts.append(jnp.where(riota == i2, 1.0, 0.0).astype(BF16))
            x = _dot_nt(jnp.concatenate(lts, axis=1), jnp.concatenate(rts, axis=1))
            words = pltpu.bitcast(x.astype(BF16), I32)
            s_s[pl.ds(pl.multiple_of(p * S_PITCH, SUBLANES), nk), :] = words

        def pair_group(pg, carry):
            for q in range(PAIR_UNROLL):
                pair(pg * PAIR_UNROLL + q)
            return carry

        lax.fori_loop(0, ntok // (2 * PAIR_UNROLL), pair_group, 0)

    h2 = h2_ref[...]
    i1_per_sub = EXP_SUB // nk
    for j in range(EXP_CHUNK // EXP_SUB):
        a = _dot_nt(h2, u_ref[j * EXP_SUB:(j + 1) * EXP_SUB, :])
        i1_0 = c * (EXP_CHUNK // nk) + j * i1_per_sub
        gs = [pltpu.bitcast(s_s[pl.ds(i1_0 + t, ntok // 2, stride=S_PITCH), :], BF16)
              for t in range(i1_per_sub)]
        g = jnp.concatenate(gs, axis=1).astype(F32)
        w2_s[:, j * EXP_SUB:(j + 1) * EXP_SUB] = (_gelu(a) * g).astype(BF16)
    acc_s[...] += _dot(w2_s[...], v_ref[...])

    @pl.when(c == nchunks - 1)
    def _():
        x2 = x1_ref[...] + _mod_rows(g2_ref, ntok) * acc_s[...]
        y_ref[...] = _rmsnorm(x2, fnorm_ref[...])


def _expert(h2, eT, gT, u_bf, v_bf, x1, mod3, fnorm, *, seq_tokens):
    n, d = x1.shape
    nt = EXP_TOK
    ne = u_bf.shape[0]
    kern = functools.partial(_expert_kernel, ntok=nt)
    return pl.pallas_call(
        kern,
        grid=(n // nt, ne // EXP_CHUNK),
        in_specs=[
            pl.BlockSpec((nt, d), lambda i, c: (i, 0)),
            pl.BlockSpec((nt // LANES, PEER_HEADS * PEER_TOPK, LANES), lambda i, c: (i, 0, 0)),
            pl.BlockSpec((nt // LANES, PEER_HEADS * PEER_TOPK, LANES), lambda i, c: (i, 0, 0)),
            pl.BlockSpec((EXP_CHUNK, d), lambda i, c: (c, 0)),
            pl.BlockSpec((EXP_CHUNK, d), lambda i, c: (c, 0)),
            pl.BlockSpec((nt, d), lambda i, c: (i, 0)),
            _mod_spec(seq_tokens, nt, d, 5, lambda i, c: i),
            pl.BlockSpec((1, d), lambda i, c: (0, 0)),
        ],
        out_specs=pl.BlockSpec((nt, d), lambda i, c: (i, 0)),
        out_shape=jax.ShapeDtypeStruct((n, d), F32),
        scratch_shapes=[
            pltpu.VMEM((nt // 2 * S_PITCH, LANES), I32),
            pltpu.VMEM((nt, PEER_HEADS * PEER_TOPK), I32),
            pltpu.VMEM((nt, PEER_HEADS * PEER_TOPK), F32),
            pltpu.VMEM((nt, EXP_CHUNK), BF16),
            pltpu.VMEM((nt, d), F32),
        ],
        compiler_params=pltpu.CompilerParams(
            dimension_semantics=("arbitrary", "arbitrary"), vmem_limit_bytes=VMEM_LIMIT),
        name="expert",
    )(h2, eT, gT, u_bf, v_bf, x1, mod3, fnorm)


def _run_group(x, mod, pool0, conv0, h0, wts, expert_tables, *, bb, seg, pos0):
    bsz, t, d = x.shape
    n = bsz * t
    mod3 = mod[:, None, :]
    cast = tuple(tb for tb in expert_tables if tb.dtype != BF16)
    x1, pool_o, conv_o, h_o, *cast_out = _mixer(
        x, mod3, pool0, conv0, h0[:, None, :], wts, bb=bb, seg=seg, pos0=pos0, tables=cast)
    u_bf, v_bf = cast_out if cast else expert_tables
    x1f = x1.reshape(n, d)
    h2, eT, gT = _route(x1f, mod3, wts["norm2"], wts["w_q"], wts["keys"], seq_tokens=t)
    y = _expert(h2, eT, gT, u_bf, v_bf, x1f, mod3, wts["fnorm"], seq_tokens=t)
    return (y.reshape(bsz, t, d), pool_o, conv_o, h_o[:, 0]), (u_bf, v_bf)


def kernel(x_prompt, x_sample, c_prompt, c_sample, state_pool, state_conv, state_lru, norm1, norm2, w_ada, b_ada, w_in, b_gate, pool_mix, pool_scale, conv_w, conv_b, w_rg, b_rg, w_ig, b_ig, lru_lambda, w_branch_a, w_branch_b, w_out, w_q, sub_keys, expert_u, expert_v, final_norm):
    depth = w_in.shape[0]
    assert depth == 1, "single-layer trunk"
    bp, tp, d = x_prompt.shape
    bs, ts, _ = x_sample.shape
    l = 0
    mod_p, mod_s = _ada(c_prompt, c_sample, w_ada[l], b_ada[l])
    row = lambda v: v.reshape(1, -1)
    wts = {
        "norm1": row(norm1[l]), "norm2": row(norm2[l]), "fnorm": row(final_norm),
        "w_in": w_in[l].astype(BF16), "b_gate": row(b_gate[l]),
        "pool_mix": pool_mix[l].astype(BF16), "pool_scale": row(pool_scale[l]),
        "conv_w": conv_w[l], "conv_b": row(conv_b[l]),
        "w_gi": jnp.concatenate([w_rg[l], w_ig[l]], axis=-1).astype(BF16),
        "b_rg": row(b_rg[l]), "b_ig": row(b_ig[l]), "lam": row(lru_lambda[l]),
        "w_a": w_branch_a[l].astype(BF16), "w_b": w_branch_b[l].astype(BF16),
        "w_out": w_out[l].astype(BF16), "w_q": w_q[l].astype(BF16),
        "keys": sub_keys[l].astype(BF16),
    }
    zp = jnp.zeros((bp, state_pool.shape[2], state_pool.shape[3]), F32)
    zc = jnp.zeros((bp, state_conv.shape[2], state_conv.shape[3]), F32)
    zh = jnp.zeros((bp, state_lru.shape[2]), F32)
    (y_p, pool_p, conv_p, lru_p), tables_bf = _run_group(
        x_prompt, mod_p, zp, zc, zh, wts, (expert_u[l], expert_v[l]), bb=1, seg=256, pos0=0)
    (y_s, pool_s, conv_s, lru_s), _ = _run_group(
        x_sample, mod_s, state_pool[l], state_conv[l], state_lru[l], wts, tables_bf,
        bb=32, seg=ts, pos0=PAST_LEN)
    st = lambda v: v[None]
    return (y_p, y_s, st(pool_p), st(conv_p), st(lru_p).astype(state_lru.dtype),
            st(pool_s), st(conv_s), st(lru_s).astype(state_lru.dtype))
```
